```python
import math
import jax, jax.numpy as jnp
from jax import lax
import numpy as np

D_MODEL = 1024
BATCH = 4
SEQ = 4096
DEPTH = 1
DEC_BATCH = 128
DEC_SEQ = 4
PAST_LEN = 8192
PAGE_SIZE = 128

FOX_HEADS = 8
FOX_DH = D_MODEL // 16
FOX_W = FOX_HEADS * FOX_DH
RET_HEADS = 4
RET_DK = D_MODEL // 16
RET_DV = 2 * RET_DK
RET_QK_W = RET_HEADS * RET_DK
RET_V_W = RET_HEADS * RET_DV
MIX_W = FOX_W + RET_V_W
D_FF = -(-8 * D_MODEL // (3 * 256)) * 256
Q_BLOCK = 128
RET_CHUNK = 128
ROPE_BASE = 10000.0
LN_EPS = 1e-5
GN_EPS = 1e-5
DEEPNORM_ALPHA = (2 * DEPTH) ** 0.25
DEEPNORM_BETA = (8 * DEPTH) ** -0.25

O_FQ = 0
O_FK = O_FQ + FOX_W
O_FV = O_FK + FOX_W
O_FF = O_FV + FOX_W
O_RQ = O_FF + FOX_HEADS
O_RK = O_RQ + RET_QK_W
O_RV = O_RK + RET_QK_W
O_RG = O_RV + RET_V_W
IN_W = O_RG + RET_V_W

kernel_name = "fox_retention_hymba_deepnorm_step"


def _layer_norm(x, g, b):
    xf = x.astype(jnp.float32)
    mu = jnp.mean(xf, axis=-1, keepdims=True)
    var = jnp.mean(jnp.square(xf - mu), axis=-1, keepdims=True)
    return ((xf - mu) * lax.rsqrt(var + LN_EPS) * g + b).astype(x.dtype)


def _rope(x, pos):
    half = x.shape[-1] // 2
    inv = ROPE_BASE ** (-jnp.arange(half, dtype=jnp.float32) / half)
    ang = pos.astype(jnp.float32)[:, None] * inv[None, :]
    cos = jnp.cos(ang)[None, :, None, :]
    sin = jnp.sin(ang)[None, :, None, :]
    xf = x.astype(jnp.float32)
    x1, x2 = xf[..., :half], xf[..., half:]
    return jnp.concatenate([x1 * cos - x2 * sin, x1 * sin + x2 * cos], axis=-1).astype(x.dtype)


def _project(x, pos, w_in, b_fgate):
    b, l, _ = x.shape
    p = jnp.einsum('bld,de->ble', x, w_in)
    qf = p[..., O_FQ:O_FK].reshape(b, l, FOX_HEADS, FOX_DH)
    kf = p[..., O_FK:O_FV].reshape(b, l, FOX_HEADS, FOX_DH)
    vf = p[..., O_FV:O_FF].reshape(b, l, FOX_HEADS, FOX_DH)
    lf = jax.nn.log_sigmoid((p[..., O_FF:O_RQ] + b_fgate).astype(jnp.float32))
    qr = _rope(p[..., O_RQ:O_RK].reshape(b, l, RET_HEADS, RET_DK), pos)
    kr = _rope(p[..., O_RK:O_RV].reshape(b, l, RET_HEADS, RET_DK), pos) * (RET_DK ** -0.5)
    vr = p[..., O_RV:O_RG].reshape(b, l, RET_HEADS, RET_DV)
    g = p[..., O_RG:IN_W]
    return qf, kf, vf, lf, qr, kr, vr, g


def _fox_prompt(q, k, v, lf):
    b, l, h, d = q.shape
    nb = l // Q_BLOCK
    scale = FOX_DH ** -0.5
    c_t = jnp.cumsum(lf, axis=1).transpose(0, 2, 1)
    qb = q.reshape(b, nb, Q_BLOCK, h, d).transpose(1, 0, 2, 3, 4)
    cb = c_t.reshape(b, h, nb, Q_BLOCK).transpose(2, 0, 1, 3)
    kpos = jnp.arange(l)

    def block(args):
        i, qi, ci = args
        s = jnp.einsum('bqhd,bkhd->bhqk', qi, k).astype(jnp.float32) * scale
        s = s + ci[..., :, None] - c_t[:, :, None, :]
        qpos = i * Q_BLOCK + jnp.arange(Q_BLOCK)
        s = jnp.where(kpos[None, :] <= qpos[:, None], s, -jnp.inf)
        pr = jax.nn.softmax(s, axis=-1)
        return jnp.einsum('bhqk,bkhd->bqhd', pr.astype(v.dtype), v)

    o = lax.map(block, (jnp.arange(nb), qb, cb))
    return o.transpose(1, 0, 2, 3, 4).reshape(b, l, h * d)


def _fox_sample(q, k, v, lf, cache_k, cache_v, cache_logf, page_table):
    bsz, t, h, d = q.shape
    scale = FOX_DH ** -0.5

    def one(args):
        qi, ki, vi, lfi, pages = args
        kp = cache_k[pages].reshape(-1, h, d)
        vp = cache_v[pages].reshape(-1, h, d)
        lfp = cache_logf[pages].reshape(-1, h).astype(jnp.float32)
        past = kp.shape[0]
        kall = jnp.concatenate([kp, ki.astype(kp.dtype)], axis=0)
        vall = jnp.concatenate([vp, vi.astype(vp.dtype)], axis=0)
        c = jnp.cumsum(jnp.concatenate([lfp, lfi], axis=0), axis=0)
        s = jnp.einsum('qhd,khd->hqk', qi, kall).astype(jnp.float32) * scale
        s = s + c[past:].T[:, :, None] - c.T[:, None, :]
        mask = jnp.arange(past + t)[None, :] <= (past + jnp.arange(t))[:, None]
        s = jnp.where(mask[None], s, -jnp.inf)
        pr = jax.nn.softmax(s, axis=-1)
        return jnp.einsum('hqk,khd->qhd', pr.astype(vall.dtype), vall)

    o = lax.map(one, (q, k, v, lf, page_table))
    return o.reshape(bsz, t, h * d).astype(q.dtype)


def _retention(q, k, v, s0):
    b, l, h, dk = q.shape
    dv = v.shape[-1]
    chunk = RET_CHUNK if l % RET_CHUNK == 0 else l
    n = l // chunk
    log_g = jnp.log1p(-(2.0 ** (-5.0 - jnp.arange(h, dtype=jnp.float32))))
    idx = jnp.arange(chunk, dtype=jnp.float32)
    diff = idx[:, None] - idx[None, :]
    dmat = jnp.where(diff[None] >= 0,
                     jnp.exp(jnp.maximum(diff, 0.0)[None] * log_g[:, None, None]), 0.0)
    xi = jnp.exp((idx + 1.0)[:, None] * log_g[None, :])
    zeta = jnp.exp((chunk - 1.0 - idx)[:, None] * log_g[None, :])
    gc = jnp.exp(chunk * log_g)

    def to_chunks(a):
        return a.astype(jnp.float32).reshape(b, n, chunk, h, a.shape[-1]).transpose(1, 0, 2, 3, 4)

    def step(s, inp):
        qi, ki, vi = inp
        inner = jnp.einsum('bqhd,bkhd->bhqk', qi, ki) * dmat[None]
        o = (jnp.einsum('bhqk,bkhe->bqhe', inner, vi)
             + jnp.einsum('bqhd,bhde->bqhe', qi, s) * xi[None, :, :, None])
        s = s * gc[None, :, None, None] + jnp.einsum('bkhd,bkhe->bhde', ki * zeta[None, :, :, None], vi)
        return s, o

    s, o = lax.scan(step, s0.astype(jnp.float32), (to_chunks(q), to_chunks(k), to_chunks(v)))
    return o.transpose(1, 0, 2, 3, 4).reshape(b, l, h, dv), s


def _finish(x, o_fox, o_ret, g, ret_gn_g, ret_gn_b, w_o, ln1_g, ln1_b, w_gate, w_up, w_down, ln2_g, ln2_b):
    b, l, _ = x.shape
    mu = jnp.mean(o_ret, axis=-1, keepdims=True)
    var = jnp.mean(jnp.square(o_ret - mu), axis=-1, keepdims=True)
    yr = ((o_ret - mu) * lax.rsqrt(var + GN_EPS)).reshape(b, l, RET_V_W) * ret_gn_g + ret_gn_b
    yr = (jax.nn.silu(g.astype(jnp.float32)) * yr).astype(x.dtype)
    mix = jnp.einsum('ble,ed->bld', jnp.concatenate([o_fox.astype(x.dtype), yr], axis=-1), w_o)
    h = _layer_norm(DEEPNORM_ALPHA * x + mix, ln1_g, ln1_b)
    f = jnp.einsum('blf,fd->bld',
                   jax.nn.silu(jnp.einsum('bld,df->blf', h, w_gate)) * jnp.einsum('bld,df->blf', h, w_up),
                   w_down)
    return _layer_norm(DEEPNORM_ALPHA * h + f, ln2_g, ln2_b)


def setup_inputs(seed: int = 0) -> dict:
    key = jax.random.key(seed)
    ks = jax.random.split(key, 24)
    n_pages = PAST_LEN // PAGE_SIZE
    n_used = DEC_BATCH * n_pages
    n_pool = n_used + max(1, n_used // 4)
    f32 = jnp.float32

    x_prompt = jax.random.normal(ks[0], (BATCH, SEQ, D_MODEL), f32)
    x_sample = jax.random.normal(ks[1], (DEC_BATCH, DEC_SEQ, D_MODEL), f32)
    cache_k = jax.random.normal(ks[2], (DEPTH, n_pool, PAGE_SIZE, FOX_HEADS, FOX_DH), f32)
    cache_v = jax.random.normal(ks[3], (DEPTH, n_pool, PAGE_SIZE, FOX_HEADS, FOX_DH), f32) * DEEPNORM_BETA
    cache_logf = jax.nn.log_sigmoid(
        3.0 + jax.random.normal(ks[4], (DEPTH, n_pool, PAGE_SIZE, FOX_HEADS), f32))
    state_ret = 0.5 * jax.random.normal(ks[5], (DEPTH, DEC_BATCH, RET_HEADS, RET_DK, RET_DV), f32)
    page_table = jax.random.permutation(ks[6], n_pool)[:n_used].reshape(DEC_BATCH, n_pages).astype(jnp.int32)

    s_in = D_MODEL ** -0.5
    w_in = jax.random.normal(ks[7], (DEPTH, D_MODEL, IN_W), f32) * s_in
    col_scale = jnp.ones((IN_W,), f32)
    col_scale = col_scale.at[O_FV:O_FF].set(DEEPNORM_BETA).at[O_RV:O_RG].set(DEEPNORM_BETA)
    w_in = w_in * col_scale
    b_fgate = 3.0 + 0.5 * jax.random.normal(ks[8], (DEPTH, FOX_HEADS), f32)
    ret_gn_g = 1.0 + 0.05 * jax.random.normal(ks[9], (DEPTH, RET_V_W), f32)
    ret_gn_b = 0.02 * jax.random.normal(ks[10], (DEPTH, RET_V_W), f32)
    w_o = jax.random.normal(ks[11], (DEPTH, MIX_W, D_MODEL), f32) * (MIX_W ** -0.5) * DEEPNORM_BETA
    ln1_g = 1.0 + 0.05 * jax.random.normal(ks[12], (DEPTH, D_MODEL), f32)
    ln1_b = 0.02 * jax.random.normal(ks[13], (DEPTH, D_MODEL), f32)
    w_gate = jax.random.normal(ks[14], (DEPTH, D_MODEL, D_FF), f32) * s_in * DEEPNORM_BETA
    w_up = jax.random.normal(ks[15], (DEPTH, D_MODEL, D_FF), f32) * s_in * DEEPNORM_BETA
    w_down = jax.random.normal(ks[16], (DEPTH, D_FF, D_MODEL), f32) * (D_FF ** -0.5) * DEEPNORM_BETA
    ln2_g = 1.0 + 0.05 * jax.random.normal(ks[17], (DEPTH, D_MODEL), f32)
    ln2_b = 0.02 * jax.random.normal(ks[18], (DEPTH, D_MODEL), f32)
    return {"x_prompt": x_prompt, "x_sample": x_sample, "cache_k": cache_k, "cache_v": cache_v,
            "cache_logf": cache_logf, "state_ret": state_ret, "page_table": page_table,
            "w_in": w_in, "b_fgate": b_fgate, "ret_gn_g": ret_gn_g, "ret_gn_b": ret_gn_b,
            "w_o": w_o, "ln1_g": ln1_g, "ln1_b": ln1_b, "w_gate": w_gate, "w_up": w_up,
            "w_down": w_down, "ln2_g": ln2_g, "ln2_b": ln2_b}


def reference(x_prompt, x_sample, cache_k, cache_v, cache_logf, state_ret, page_table,
              w_in, b_fgate, ret_gn_g, ret_gn_b, w_o, ln1_g, ln1_b, w_gate, w_up, w_down, ln2_g, ln2_b):
    pos_p = jnp.arange(x_prompt.shape[1], dtype=jnp.int32)
    pos_s = PAST_LEN + jnp.arange(x_sample.shape[1], dtype=jnp.int32)
    hp, hs = x_prompt, x_sample
    kp_l, vp_l, lfp_l, sp_l, ks_l, vs_l, lfs_l, ss_l = [], [], [], [], [], [], [], []
    for i in range(DEPTH):
        tail = (ret_gn_g[i], ret_gn_b[i], w_o[i], ln1_g[i], ln1_b[i],
                w_gate[i], w_up[i], w_down[i], ln2_g[i], ln2_b[i])
        qf, kf, vf, lf, qr, kr, vr, g = _project(hp, pos_p, w_in[i], b_fgate[i])
        o_f = _fox_prompt(qf, kf, vf, lf)
        s0 = jnp.zeros((hp.shape[0], RET_HEADS, RET_DK, RET_DV), jnp.float32)
        o_r, s_p = _retention(qr, kr, vr, s0)
        kp_l.append(kf); vp_l.append(vf); lfp_l.append(lf); sp_l.append(s_p.astype(state_ret.dtype))
        new_hp = _finish(hp, o_f, o_r, g, *tail)
        qf, kf, vf, lf, qr, kr, vr, g = _project(hs, pos_s, w_in[i], b_fgate[i])
        o_f = _fox_sample(qf, kf, vf, lf, cache_k[i], cache_v[i], cache_logf[i], page_table)
        o_r, s_s = _retention(qr, kr, vr, state_ret[i])
        ks_l.append(kf); vs_l.append(vf); lfs_l.append(lf); ss_l.append(s_s.astype(state_ret.dtype))
        hs = _finish(hs, o_f, o_r, g, *tail)
        hp = new_hp
    return (hp, hs, jnp.stack(kp_l), jnp.stack(vp_l), jnp.stack(lfp_l), jnp.stack(sp_l),
            jnp.stack(ks_l), jnp.stack(vs_l), jnp.stack(lfs_l), jnp.stack(ss_l))
```

```python
import functools

import jax
import jax.numpy as jnp
import numpy as np
from jax import lax
from jax.experimental import pallas as pl
from jax.experimental.pallas import tpu as pltpu

ROPE_BASE = 10000.0
LN_EPS = 1e-5
GN_EPS = 1e-5
RET_CHUNK = 128

LANES = 128
VMEM_LIMIT_BYTES = 56 * 1024 * 1024

BF16 = jnp.bfloat16
F32 = jnp.float32
NEG_INF = float("-inf")


def _params(semantics):
    return pltpu.CompilerParams(dimension_semantics=semantics, vmem_limit_bytes=VMEM_LIMIT_BYTES)


def _dot(a, b):
    return jnp.dot(a, b, preferred_element_type=F32)


def _dot_nt(a, b):
    return lax.dot_general(a, b, (((1,), (1,)), ((), ())), preferred_element_type=F32)


def _lane_cumsum(x):
    lane = lax.broadcasted_iota(jnp.int32, x.shape, 1)
    shift = 1
    while shift < LANES:
        x = x + jnp.where(lane >= shift, pltpu.roll(x, shift, axis=1), 0.0)
        shift *= 2
    return x


def _silu(x):
    return x / (1.0 + jnp.exp(-x))


def _layer_norm(x, g, b):
    mu = jnp.mean(x, axis=-1, keepdims=True)
    xc = x - mu
    var = jnp.mean(xc * xc, axis=-1, keepdims=True)
    return xc * lax.rsqrt(var + LN_EPS) * g + b


def _proj_kernel(x_ref, wfox_ref, wfg_ref, wret_ref, bfg_ref, cos_ref, sin_ref, *refs,
                 fox_w, fox_heads, ret_qk_w, q_scale, k_scale, prompt):
    if prompt:
        (qf_ref, kt_ref, vt_ref, ktb_ref, vfb_ref, lft_ref, ct_ref,
         qr_ref, kr_ref, vr_ref, g_ref, carry_ref) = refs
    else:
        qf_ref, kf_ref, vf_ref, lf_ref, qr_ref, kr_ref, vr_ref, g_ref = refs
    xb = x_ref[...].astype(BF16)
    tm = xb.shape[0]

    pf = _dot(xb, wfox_ref[...])
    kf = pf[:, fox_w:2 * fox_w]
    vf = pf[:, 2 * fox_w:3 * fox_w]
    z = _dot(xb, wfg_ref[...]) + bfg_ref[...]
    lf = jnp.minimum(z, 0.0) - jnp.log1p(jnp.exp(-jnp.abs(z)))
    if prompt:
        qf_ref[...] = (pf[:, :fox_w] * q_scale).astype(BF16)
        kt = kf.T
        kt_ref[...] = kt
        ktb_ref[...] = kt.astype(BF16)
        vt_ref[...] = vf.T
        vfb_ref[...] = vf.astype(BF16)

        @pl.when(pl.program_id(1) == 0)
        def _():
            carry_ref[...] = jnp.zeros_like(carry_ref)

        lft = lf.T[:fox_heads, :]
        lft_ref[...] = lft
        off = carry_ref[...]
        for v in range(tm // LANES):
            blk = _lane_cumsum(lft[:, v * LANES:(v + 1) * LANES]) + off
            ct_ref[:, v * LANES:(v + 1) * LANES] = blk
            off = jnp.broadcast_to(blk[:, LANES - 1:LANES], blk.shape)
        carry_ref[...] = off
    else:
        qf_ref[...] = pf[:, :fox_w] * q_scale
        kf_ref[...] = kf
        vf_ref[...] = vf
        lf_ref[...] = lf[:, :fox_heads]

    pr = _dot(xb, wret_ref[...])
    cos = cos_ref[...]
    sin = sin_ref[...]
    lane = lax.broadcasted_iota(jnp.int32, (tm, LANES), 1)
    first_half = (lane & 32) == 0

    def rope(xh):
        partner = jnp.where(first_half, pltpu.roll(xh, LANES - 32, axis=1), pltpu.roll(xh, 32, axis=1))
        return xh * cos + partner * sin

    for j in range(ret_qk_w // LANES):
        sl = slice(j * LANES, (j + 1) * LANES)
        qr_ref[:, sl] = rope(pr[:, sl]).astype(BF16)
        kr_ref[:, sl] = rope(pr[:, ret_qk_w + j * LANES:ret_qk_w + (j + 1) * LANES]) * k_scale
    v_w = (pr.shape[1] - 2 * ret_qk_w) // 2
    vr_ref[...] = pr[:, 2 * ret_qk_w:2 * ret_qk_w + v_w].astype(BF16)
    g_ref[...] = pr[:, 2 * ret_qk_w + v_w:]


def _projection(x, wfox, wfg, wret, bfg, cos_tab, sin_tab, *, fox_heads, fox_dh, ret_heads, ret_dk, prompt, tm):
    b, l, d = x.shape
    fox_w = fox_heads * fox_dh
    ret_qk_w = ret_heads * ret_dk
    ret_v_w = (wret.shape[1] - 2 * ret_qk_w) // 2
    tm = min(tm, l)
    grid = (b, l // tm)

    def row_spec(w):
        return pl.BlockSpec((None, tm, w), lambda i, j: (i, j, 0))

    def col_spec(w):
        return pl.BlockSpec((None, w, tm), lambda i, j: (i, 0, j))

    def const_spec(shape):
        return pl.BlockSpec(shape, lambda i, j: (0,) * len(shape))

    in_specs = [row_spec(d), const_spec(wfox.shape), const_spec(wfg.shape), const_spec(wret.shape),
                const_spec(bfg.shape),
                pl.BlockSpec((tm, LANES), lambda i, j: (j, 0)),
                pl.BlockSpec((tm, LANES), lambda i, j: (j, 0))]

    def rows(w, dt):
        return jax.ShapeDtypeStruct((b, l, w), dt)

    def cols(w, dt):
        return jax.ShapeDtypeStruct((b, w, l), dt)

    ret_shapes = [rows(ret_qk_w, BF16), rows(ret_qk_w, F32), rows(ret_v_w, BF16), rows(ret_v_w, F32)]
    ret_specs = [row_spec(ret_qk_w), row_spec(ret_qk_w), row_spec(ret_v_w), row_spec(ret_v_w)]
    if prompt:
        out_shape = [rows(fox_w, BF16), cols(fox_w, F32), cols(fox_w, F32), cols(fox_w, BF16), rows(fox_w, BF16),
                     cols(fox_heads, F32), cols(fox_heads, F32)] + ret_shapes
        out_specs = [row_spec(fox_w), col_spec(fox_w), col_spec(fox_w), col_spec(fox_w), row_spec(fox_w),
                     col_spec(fox_heads), col_spec(fox_heads)] + ret_specs
        scratch = [pltpu.VMEM((fox_heads, LANES), F32)]
    else:
        out_shape = [rows(fox_w, F32), rows(fox_w, F32), rows(fox_w, F32), rows(fox_heads, F32)] + ret_shapes
        out_specs = [row_spec(fox_w)] * 3 + [row_spec(fox_heads)] + ret_specs
        scratch = []
    kern = functools.partial(_proj_kernel, fox_w=fox_w, fox_heads=fox_heads, ret_qk_w=ret_qk_w,
                             q_scale=fox_dh ** -0.5, k_scale=ret_dk ** -0.5, prompt=prompt)
    return pl.pallas_call(
        kern, grid=grid, in_specs=in_specs, out_specs=out_specs, out_shape=out_shape,
        scratch_shapes=scratch, compiler_params=_params(("parallel", "arbitrary")),
        name="projection_prompt" if prompt else "projection_sample",
    )(x, wfox, wfg, wret, bfg, cos_tab, sin_tab)


def _fox_prompt_kernel(q_ref, kt_ref, v_ref, c_ref, o_ref, m_ref, l_ref, acc_ref, *, dh, tq):
    qi = pl.program_id(2)
    q = q_ref[...]
    lane = lax.broadcasted_iota(jnp.int32, q.shape, 1)
    in_a = lane < dh
    zero = jnp.zeros_like(q)
    q_heads = (jnp.where(in_a, q, zero), jnp.where(in_a, zero, q))

    m_ref[...] = jnp.full_like(m_ref, NEG_INF)
    l_ref[...] = jnp.zeros_like(l_ref)
    acc_ref[...] = jnp.zeros_like(acc_ref)

    def block(kj, masked):
        off = pl.multiple_of(kj * tq, tq)
        ktb = kt_ref[:, pl.ds(off, tq)]
        vb = v_ref[pl.ds(off, tq), :]
        cb = c_ref[:, pl.ds(off, tq)]
        for h in range(2):
            s = _dot(q_heads[h], ktb) - cb[h:h + 1, :]
            if masked:
                row = lax.broadcasted_iota(jnp.int32, s.shape, 0)
                col = lax.broadcasted_iota(jnp.int32, s.shape, 1)
                s = jnp.where(col <= row, s, NEG_INF)
            m_old = m_ref[h]
            m_new = jnp.maximum(m_old, jnp.max(s, axis=1, keepdims=True))
            alpha = jnp.exp(m_old - m_new)
            p = jnp.exp(s - m_new)
            l_ref[h] = alpha * l_ref[h] + jnp.sum(p, axis=1, keepdims=True)
            acc_ref[h] = alpha * acc_ref[h] + _dot(p.astype(BF16), vb)
            m_ref[h] = m_new

    def body(kj, carry):
        block(kj, False)
        return carry

    lax.fori_loop(0, qi, body, 0)
    block(qi, True)
    o_a = acc_ref[0] / l_ref[0]
    o_b = acc_ref[1] / l_ref[1]
    o_ref[...] = jnp.where(in_a, o_a, o_b).astype(o_ref.dtype)


def _fox_prompt(qb, ktb, vb, ct, *, heads, dh, tq):
    b, l, w = qb.shape
    assert 2 * dh == LANES and heads % 2 == 0
    tq = min(tq, l)
    pairs = heads // 2
    c4 = ct.reshape(b, pairs, 2, l)
    grid = (b, pairs, l // tq)
    return pl.pallas_call(
        functools.partial(_fox_prompt_kernel, dh=dh, tq=tq),
        grid=grid,
        in_specs=[pl.BlockSpec((None, tq, LANES), lambda i, p, j: (i, j, p)),
                  pl.BlockSpec((None, LANES, l), lambda i, p, j: (i, p, 0)),
                  pl.BlockSpec((None, l, LANES), lambda i, p, j: (i, 0, p)),
                  pl.BlockSpec((None, None, 2, l), lambda i, p, j: (i, p, 0, 0))],
        out_specs=pl.BlockSpec((None, tq, LANES), lambda i, p, j: (i, j, p)),
        out_shape=jax.ShapeDtypeStruct((b, l, w), BF16),
        scratch_shapes=[pltpu.VMEM((2, tq, 1), F32), pltpu.VMEM((2, tq, 1), F32),
                        pltpu.VMEM((2, tq, LANES), F32)],
        compiler_params=_params(("parallel", "parallel", "arbitrary")),
        name="fox_prompt_attention",
    )(qb, ktb, vb, c4)


def _group_norm_gate(o, gate, gn_g, gn_b):
    mu = jnp.mean(o, axis=-1, keepdims=True)
    oc = o - mu
    var = jnp.mean(oc * oc, axis=-1, keepdims=True)
    y = oc * lax.rsqrt(var + GN_EPS) * gn_g + gn_b
    return _silu(gate) * y


def _ret_prompt_kernel(q_ref, k_ref, v_ref, g_ref, dmat_ref, xi_ref, zeta_ref, gng_ref, gnb_ref,
                       y_ref, s_out_ref, s_ref, *, heads, dk, dv, chunk, gc):
    @pl.when(pl.program_id(1) == 0)
    def _():
        s_ref[...] = jnp.zeros_like(s_ref)

    rows = q_ref.shape[0]
    lane = lax.broadcasted_iota(jnp.int32, (chunk, heads * dk), 1)
    for c in range(rows // chunk):
        r = slice(c * chunk, (c + 1) * chunk)
        q = q_ref[r, :]
        k = k_ref[r, :]
        kb = k.astype(BF16)
        kzt = (k * zeta_ref[...]).T.astype(BF16)
        s_old = s_ref[...]
        s_old_b = s_old.astype(BF16)
        for h in range(heads):
            qh = jnp.where((lane >= h * dk) & (lane < (h + 1) * dk), q, jnp.zeros_like(q))
            vh = v_ref[r, h * dv:(h + 1) * dv]
            inner = _dot_nt(qh, kb) * dmat_ref[h]
            o = _dot(inner.astype(BF16), vh) + _dot(qh, s_old_b) * xi_ref[h]
            hs = slice(h * dk, (h + 1) * dk)
            s_ref[hs, :] = gc[h] * s_old[hs, :] + _dot(kzt[hs, :], vh)
            cs = slice(h * dv, (h + 1) * dv)
            y = _group_norm_gate(o, g_ref[r, cs], gng_ref[:, cs], gnb_ref[:, cs])
            y_ref[r, cs] = y.astype(y_ref.dtype)
    s_out_ref[...] = s_ref[...]


def _ret_prompt(qr, kr, vr, g, dmat, xi, zeta, gn_g, gn_b, *, heads, dk, dv, gc, tr):
    b, l, _ = qr.shape
    chunk = dmat.shape[-1]
    tr = min(tr, l)
    grid = (b, l // tr)

    def row_spec(w):
        return pl.BlockSpec((None, tr, w), lambda i, j: (i, j, 0))

    def const_spec(shape):
        return pl.BlockSpec(shape, lambda i, j: (0,) * len(shape))

    return pl.pallas_call(
        functools.partial(_ret_prompt_kernel, heads=heads, dk=dk, dv=dv, chunk=chunk, gc=gc),
        grid=grid,
        in_specs=[row_spec(heads * dk), row_spec(heads * dk), row_spec(heads * dv), row_spec(heads * dv),
                  const_spec(dmat.shape), const_spec(xi.shape), const_spec(zeta.shape),
                  const_spec(gn_g.shape), const_spec(gn_b.shape)],
        out_specs=[row_spec(heads * dv), pl.BlockSpec((None, heads * dk, dv), lambda i, j: (i, 0, 0))],
        out_shape=[jax.ShapeDtypeStruct((b, l, heads * dv), BF16),
                   jax.ShapeDtypeStruct((b, heads * dk, dv), F32)],
        scratch_shapes=[pltpu.VMEM((heads * dk, dv), F32)],
        compiler_params=_params(("parallel", "arbitrary")),
        name="retention_prompt",
    )(qr, kr, vr, g, dmat, xi, zeta, gn_g, gn_b)


def _ret_sample_kernel(q_ref, k_ref, v_ref, g_ref, st_ref, dm_ref, xi_ref, zeta_ref, gc_ref, gng_ref, gnb_ref,
                       y_ref, st_out_ref, *, heads, dk, dv, seqs, t):
    rows = seqs * t
    q = q_ref[...]
    k = k_ref[...]
    kz = k * zeta_ref[...]
    kb = k.astype(BF16)
    lane = lax.broadcasted_iota(jnp.int32, q.shape, 1)

    def head_mask(h):
        return (lane >= h * dk) & (lane < (h + 1) * dk)

    q_all = jnp.concatenate([jnp.where(head_mask(h), q, jnp.zeros_like(q)) for h in range(heads)], axis=0)
    k_all = jnp.concatenate([jnp.where(head_mask(h), kb, jnp.zeros_like(kb)) for h in range(heads)], axis=0)
    kz_all = jnp.concatenate([jnp.where(head_mask(h), kz, jnp.zeros_like(kz)) for h in range(heads)], axis=0)
    v_all = jnp.concatenate([v_ref[:, h * dv:(h + 1) * dv] for h in range(heads)], axis=0)

    inner = _dot_nt(q_all, k_all) * dm_ref[...]
    o = _dot(inner.astype(BF16), v_all)
    kzt = kz_all.T
    n = heads * rows
    rid = lax.broadcasted_iota(jnp.int32, (n, dv), 0)
    cid = lax.broadcasted_iota(jnp.int32, kzt.shape, 1)

    def of_seq(idx, s):
        hit = (idx >= s * t) & (idx < (s + 1) * t)
        for h in range(1, heads):
            hit = hit | ((idx >= h * rows + s * t) & (idx < h * rows + (s + 1) * t))
        return hit

    o_state = jnp.zeros((n, dv), F32)
    for s in range(seqs):
        s0 = st_ref[s]
        o_state = o_state + jnp.where(of_seq(rid, s), _dot(q_all, s0.astype(BF16)), 0.0)
        ds = _dot(jnp.where(of_seq(cid, s), kzt, 0.0).astype(BF16), v_all)
        st_out_ref[s] = gc_ref[...] * s0 + ds
    o = o + o_state * xi_ref[...]

    mu = jnp.mean(o, axis=-1, keepdims=True)
    oc = o - mu
    var = jnp.mean(oc * oc, axis=-1, keepdims=True)
    yn = oc * lax.rsqrt(var + GN_EPS)
    for h in range(heads):
        cs = slice(h * dv, (h + 1) * dv)
        y = yn[h * rows:(h + 1) * rows, :] * gng_ref[:, cs] + gnb_ref[:, cs]
        y_ref[:, cs] = (_silu(g_ref[:, cs]) * y).astype(y_ref.dtype)


def _ret_sample(qr, kr, vr, g, state, dm, xi, zeta, gc, gn_g, gn_b, *, heads, dk, dv, t, seqs):
    n_rows = qr.shape[0]
    n_seq = n_rows // t
    rows = seqs * t
    grid = (n_seq // seqs,)

    def row_spec(w):
        return pl.BlockSpec((rows, w), lambda i: (i, 0))

    def const_spec(shape):
        return pl.BlockSpec(shape, lambda i: (0,) * len(shape))

    st_spec = pl.BlockSpec((seqs, heads * dk, dv), lambda i: (i, 0, 0))
    return pl.pallas_call(
        functools.partial(_ret_sample_kernel, heads=heads, dk=dk, dv=dv, seqs=seqs, t=t),
        grid=grid,
        in_specs=[row_spec(heads * dk), row_spec(heads * dk), row_spec(heads * dv), row_spec(heads * dv),
                  st_spec, const_spec(dm.shape), const_spec(xi.shape), const_spec(zeta.shape),
                  const_spec(gc.shape), const_spec(gn_g.shape), const_spec(gn_b.shape)],
        out_specs=[row_spec(heads * dv), st_spec],
        out_shape=[jax.ShapeDtypeStruct((n_rows, heads * dv), BF16),
                   jax.ShapeDtypeStruct(state.shape, F32)],
        compiler_params=_params(("parallel",)),
        name="retention_sample",
    )(qr, kr, vr, g, state, dm, xi, zeta, gc, gn_g, gn_b)


def _fox_sample_kernel(pt_ref, q_ref, kn_ref, vn_ref, lfn_ref, *refs, pps, heads, dh, t, page):
    kt_refs = refs[:pps]
    vt_refs = refs[pps:2 * pps]
    lf_refs = refs[2 * pps:3 * pps]
    o_ref = refs[3 * pps]
    qbd_ref, qbdb_ref, m_ref, l_ref, acc_ref, carry_ref = refs[3 * pps + 1:]
    j = pl.program_id(1)
    w = heads * dh
    head_rows = lax.broadcasted_iota(jnp.int32, (heads, w), 0)
    head_lanes = lax.broadcasted_iota(jnp.int32, (heads, w), 1)
    own = (head_lanes >= head_rows * dh) & (head_lanes < (head_rows + 1) * dh)

    def tile_tokens(a):
        return jnp.concatenate([a] * t, axis=0)

    @pl.when(j == 0)
    def _():
        q = q_ref[...]
        qbd = jnp.concatenate(
            [jnp.where(own, jnp.broadcast_to(q[i:i + 1, :], (heads, w)), 0.0) for i in range(t)], axis=0)
        qbd_ref[...] = qbd
        qbdb_ref[...] = qbd.astype(BF16)
        m_ref[...] = jnp.full_like(m_ref, NEG_INF)
        l_ref[...] = jnp.zeros_like(l_ref)
        acc_ref[...] = jnp.zeros_like(acc_ref)
        carry_ref[...] = jnp.zeros_like(carry_ref)

    qbdb = qbdb_ref[...]
    carry = carry_ref[...]
    s_parts = []
    for i in range(pps):
        wc = _lane_cumsum(lf_refs[i][...])
        s_parts.append(_dot(qbdb, kt_refs[i][...].astype(BF16)) - tile_tokens(carry + wc))
        carry = carry + jnp.broadcast_to(wc[:, page - 1:page], wc.shape)
    carry_ref[...] = carry
    s = jnp.concatenate(s_parts, axis=1)
    m_old = m_ref[...]
    m_new = jnp.maximum(m_old, jnp.max(s, axis=1, keepdims=True))
    alpha = jnp.exp(m_old - m_new)
    p = jnp.exp(s - m_new)
    vt = jnp.concatenate([vt_refs[i][...].astype(BF16) for i in range(pps)], axis=1)
    l_ref[...] = alpha * l_ref[...] + jnp.sum(p, axis=1, keepdims=True)
    acc_ref[...] = alpha * acc_ref[...] + _dot_nt(p.astype(BF16), vt)
    m_ref[...] = m_new

    @pl.when(j == pl.num_programs(1) - 1)
    def _():
        qbd = qbd_ref[...]
        kn = kn_ref[...]
        vn = vn_ref[...]
        bias = carry_ref[...] + _lane_cumsum(lfn_ref[...])
        row = lax.broadcasted_iota(jnp.int32, (t * heads, 1), 0)
        cols = []
        for u in range(t):
            s_u = jnp.sum(qbd * kn[u:u + 1, :], axis=1, keepdims=True) - tile_tokens(bias[:, u:u + 1])
            cols.append(jnp.where(row >= u * heads, s_u, NEG_INF))
        m_old = m_ref[...]
        m_new = m_old
        for s_u in cols:
            m_new = jnp.maximum(m_new, s_u)
        alpha = jnp.exp(m_old - m_new)
        l_new = alpha * l_ref[...]
        acc = alpha * acc_ref[...]
        for u, s_u in enumerate(cols):
            p_u = jnp.exp(s_u - m_new)
            l_new = l_new + p_u
            acc = acc + p_u * vn[u:u + 1, :]
        o = acc / l_new
        o = jnp.where(own[None], o.reshape(t, heads, w), 0.0)
        o_ref[...] = jnp.sum(o, axis=1).astype(o_ref.dtype)


def _fox_sample(page_table, q, k_new, v_new, lfn_pad, cache_kt, cache_vt, cache_lft, *, pps):
    n_seq, t, w = q.shape
    n_pages = page_table.shape[1]
    _, heads, page = cache_lft.shape
    dh = w // heads
    assert page == LANES and n_pages % pps == 0
    grid = (n_seq, n_pages // pps)

    def seq_spec(shape):
        return pl.BlockSpec((None,) + shape, lambda b, j, pt: (b, 0, 0))

    def page_spec(shape, i):
        return pl.BlockSpec((None,) + shape, lambda b, j, pt: (pt[b, j * pps + i], 0, 0))

    in_specs = ([seq_spec((t, w)), seq_spec((t, w)), seq_spec((t, w)), seq_spec((heads, page))]
                + [page_spec((w, page), i) for i in range(pps)]
                + [page_spec((w, page), i) for i in range(pps)]
                + [page_spec((heads, page), i) for i in range(pps)])
    grid_spec = pltpu.PrefetchScalarGridSpec(
        num_scalar_prefetch=1, grid=grid, in_specs=in_specs,
        out_specs=pl.BlockSpec((None, t, w), lambda b, j, pt: (b, 0, 0)),
        scratch_shapes=[pltpu.VMEM((t * heads, w), F32), pltpu.VMEM((t * heads, w), BF16),
                        pltpu.VMEM((t * heads, 1), F32), pltpu.VMEM((t * heads, 1), F32),
                        pltpu.VMEM((t * heads, w), F32), pltpu.VMEM((heads, page), F32)])
    return pl.pallas_call(
        functools.partial(_fox_sample_kernel, pps=pps, heads=heads, dh=dh, t=t, page=page),
        grid_spec=grid_spec,
        out_shape=jax.ShapeDtypeStruct((n_seq, t, w), BF16),
        compiler_params=_params(("parallel", "arbitrary")),
        name="fox_sample_attention",
    )(page_table, q, k_new, v_new, lfn_pad, *([cache_kt] * pps), *([cache_vt] * pps), *([cache_lft] * pps))


def _finish_kernel(x_ref, of_ref, yr_ref, wof_ref, wor_ref, g1_ref, b1_ref, wg_ref, wu_ref, wd_ref,
                   g2_ref, b2_ref, y_ref, *, alpha):
    mix = _dot(of_ref[...], wof_ref[...]) + _dot(yr_ref[...], wor_ref[...])
    h = _layer_norm(alpha * x_ref[...] + mix, g1_ref[...], b1_ref[...])
    hb = h.astype(BF16)
    act = (_silu(_dot(hb, wg_ref[...])) * _dot(hb, wu_ref[...])).astype(BF16)
    f = _dot(act, wd_ref[...])
    y_ref[...] = _layer_norm(alpha * h + f, g2_ref[...], b2_ref[...])


def _finish(x, o_fox, yr, wof, wor, ln1_g, ln1_b, w_gate, w_up, w_down, ln2_g, ln2_b, *, alpha, tm):
    n, d = x.shape
    tm = min(tm, n)

    def row_spec(w):
        return pl.BlockSpec((tm, w), lambda i: (i, 0))

    def const_spec(a):
        return pl.BlockSpec(a.shape, lambda i: (0,) * a.ndim, pipeline_mode=pl.Buffered(1))

    consts = (wof, wor, ln1_g, ln1_b, w_gate, w_up, w_down, ln2_g, ln2_b)
    return pl.pallas_call(
        functools.partial(_finish_kernel, alpha=alpha),
        grid=(n // tm,),
        in_specs=[row_spec(d), row_spec(o_fox.shape[1]), row_spec(yr.shape[1])] + [const_spec(a) for a in consts],
        out_specs=row_spec(d),
        out_shape=jax.ShapeDtypeStruct((n, d), F32),
        compiler_params=_params(("parallel",)),
        name="finish",
    )(x, o_fox, yr, *consts)


def _rope_tables(pos, dk):
    half = dk // 2
    inv = ROPE_BASE ** (-jnp.arange(half, dtype=F32) / half)
    ang = pos.astype(F32)[:, None] * inv[None, :]
    cos = jnp.cos(ang)
    sin = jnp.sin(ang)
    reps = LANES // dk
    cos_row = jnp.tile(jnp.concatenate([cos, cos], axis=1), (1, reps))
    sin_row = jnp.tile(jnp.concatenate([-sin, sin], axis=1), (1, reps))
    return cos_row, sin_row


def _decay_tables(heads, chunk):
    log_g = jnp.log1p(-(2.0 ** (-5.0 - jnp.arange(heads, dtype=F32))))
    idx = jnp.arange(chunk, dtype=F32)
    diff = idx[:, None] - idx[None, :]
    dmat = jnp.where(diff[None] >= 0, jnp.exp(jnp.maximum(diff, 0.0)[None] * log_g[:, None, None]), 0.0)
    xi = jnp.exp((idx + 1.0)[:, None] * log_g[None, :])
    zeta = jnp.exp((chunk - 1.0 - idx)[:, None] * log_g[None, :])
    gc = jnp.exp(chunk * log_g)
    return dmat, xi, zeta, gc


def kernel(x_prompt, x_sample, cache_k, cache_v, cache_logf, state_ret, page_table, w_in, b_fgate, ret_gn_g,
           ret_gn_b, w_o, ln1_g, ln1_b, w_gate, w_up, w_down, ln2_g, ln2_b):
    depth, d_model, _ = w_in.shape
    batch, seq, _ = x_prompt.shape
    n_seq, t_dec, _ = x_sample.shape
    _, n_pool, page, fox_heads, fox_dh = cache_k.shape
    _, _, ret_heads, ret_dk, ret_dv = state_ret.shape
    fox_w = fox_heads * fox_dh
    ret_qk_w = ret_heads * ret_dk
    ret_v_w = ret_heads * ret_dv
    past_len = page_table.shape[1] * page
    alpha = (2 * depth) ** 0.25
    o_ff = 3 * fox_w
    o_rq = o_ff + fox_heads
    assert LANES % ret_dk == 0 and ret_qk_w % LANES == 0 and seq % RET_CHUNK == 0 and t_dec < RET_CHUNK

    cos_p, sin_p = _rope_tables(jnp.arange(seq, dtype=jnp.int32), ret_dk)
    pos_s = past_len + jnp.arange(t_dec, dtype=jnp.int32)
    cos_s, sin_s = _rope_tables(jnp.tile(pos_s, n_seq), ret_dk)
    dmat_p, xi_p, zeta_p, _ = _decay_tables(ret_heads, RET_CHUNK)
    xi_p_tab = jnp.broadcast_to(xi_p.T[:, :, None], (ret_heads, RET_CHUNK, ret_dv))
    zeta_p_tab = jnp.repeat(zeta_p, ret_dk, axis=1)
    log_g_np = np.log1p(-(2.0 ** (-5.0 - np.arange(ret_heads, dtype=np.float32)))).astype(np.float32)
    gc_p_static = tuple(float(v) for v in np.exp(np.float32(RET_CHUNK) * log_g_np))

    seqs = 8
    while n_seq % seqs:
        seqs //= 2
    rows = seqs * t_dec
    dmat_s, xi_s, zeta_s, gc_s = _decay_tables(ret_heads, t_dec)
    r_h = jnp.repeat(jnp.arange(ret_heads), rows)
    r_s = jnp.tile(jnp.repeat(jnp.arange(seqs), t_dec), ret_heads)
    r_t = jnp.tile(jnp.arange(t_dec), ret_heads * seqs)
    dm_s_tab = jnp.where((r_h[:, None] == r_h[None, :]) & (r_s[:, None] == r_s[None, :]),
                         dmat_s[r_h[:, None], r_t[:, None], r_t[None, :]], 0.0)
    xi_s_tab = jnp.broadcast_to(xi_s[r_t, r_h][:, None], (ret_heads * rows, ret_dv))
    zeta_s_tab = jnp.tile(jnp.repeat(zeta_s, ret_dk, axis=1), (seqs, 1))
    gc_s_tab = jnp.broadcast_to(jnp.repeat(gc_s, ret_dk)[:, None], (ret_qk_w, ret_dv))

    pps = 8
    while page_table.shape[1] % pps:
        pps //= 2

    hp = x_prompt
    hs = x_sample.reshape(1, n_seq * t_dec, d_model)
    kp_l, vp_l, lfp_l, sp_l, ks_l, vs_l, lfs_l, ss_l = [], [], [], [], [], [], [], []
    for i in range(depth):
        w = w_in[i]
        wfox = w[:, :o_ff].astype(BF16)
        wfg = jnp.pad(w[:, o_ff:o_rq], ((0, 0), (0, LANES - fox_heads))).astype(BF16)
        wret = w[:, o_rq:].astype(BF16)
        bfg = jnp.pad(b_fgate[i], (0, LANES - fox_heads))[None, :]
        wof = w_o[i][:fox_w].astype(BF16)
        wor = w_o[i][fox_w:].astype(BF16)
        gn_g = ret_gn_g[i][None, :]
        gn_b = ret_gn_b[i][None, :]
        tail = (wof, wor, ln1_g[i][None, :], ln1_b[i][None, :], w_gate[i].astype(BF16), w_up[i].astype(BF16),
                w_down[i].astype(BF16), ln2_g[i][None, :], ln2_b[i][None, :])
        proj = functools.partial(_projection, fox_heads=fox_heads, fox_dh=fox_dh, ret_heads=ret_heads,
                                 ret_dk=ret_dk, tm=512)

        qfb, kt, vt, ktb, vfb, lft, ct, qr, kr, vr, g = proj(hp, wfox, wfg, wret, bfg, cos_p, sin_p, prompt=True)
        o_f = _fox_prompt(qfb, ktb, vfb, ct, heads=fox_heads, dh=fox_dh, tq=512)
        yr, s_p = _ret_prompt(qr, kr, vr, g, dmat_p, xi_p_tab, zeta_p_tab, gn_g, gn_b, heads=ret_heads,
                              dk=ret_dk, dv=ret_dv, gc=gc_p_static, tr=512)
        kp_l.append(jnp.transpose(kt.reshape(batch, fox_heads, fox_dh, seq), (0, 3, 1, 2)))
        vp_l.append(jnp.transpose(vt.reshape(batch, fox_heads, fox_dh, seq), (0, 3, 1, 2)))
        lfp_l.append(jnp.transpose(lft, (0, 2, 1)))
        sp_l.append(s_p.reshape(batch, ret_heads, ret_dk, ret_dv).astype(state_ret.dtype))
        new_hp = _finish(hp.reshape(batch * seq, d_model), o_f.reshape(batch * seq, fox_w),
                         yr.reshape(batch * seq, ret_v_w), *tail, alpha=alpha, tm=512)

        qf, kf, vf, lf, qr, kr, vr, g = proj(hs, wfox, wfg, wret, bfg, cos_s, sin_s, prompt=False)
        cache_kt = jnp.transpose(cache_k[i], (0, 2, 3, 1)).reshape(n_pool, fox_w, page)
        cache_vt = jnp.transpose(cache_v[i], (0, 2, 3, 1)).reshape(n_pool, fox_w, page)
        cache_lft = jnp.transpose(cache_logf[i], (0, 2, 1))
        lfn = jnp.swapaxes(lf.reshape(n_seq, t_dec, fox_heads), 1, 2)
        lfn_pad = jnp.pad(lfn, ((0, 0), (0, 0), (0, page - t_dec)))
        o_f = _fox_sample(page_table, qf.reshape(n_seq, t_dec, fox_w), kf.reshape(n_seq, t_dec, fox_w),
                          vf.reshape(n_seq, t_dec, fox_w), lfn_pad, cache_kt, cache_vt, cache_lft, pps=pps)
        yr, s_s = _ret_sample(qr[0], kr[0], vr[0], g[0], state_ret[i].reshape(n_seq, ret_qk_w, ret_dv),
                              dm_s_tab, xi_s_tab, zeta_s_tab, gc_s_tab, gn_g, gn_b, heads=ret_heads, dk=ret_dk,
                              dv=ret_dv, t=t_dec, seqs=seqs)
        ks_l.append(kf.reshape(n_seq, t_dec, fox_heads, fox_dh))
        vs_l.append(vf.reshape(n_seq, t_dec, fox_heads, fox_dh))
        lfs_l.append(lf.reshape(n_seq, t_dec, fox_heads))
        ss_l.append(s_s.reshape(n_seq, ret_heads, ret_dk, ret_dv).astype(state_ret.dtype))
        hs = _finish(hs[0], o_f.reshape(n_seq * t_dec, fox_w), yr, *tail, alpha=alpha,
                     tm=512).reshape(1, n_seq * t_dec, d_model)
        hp = new_hp.reshape(batch, seq, d_model)
    return (hp, hs.reshape(n_seq, t_dec, d_model), jnp.stack(kp_l), jnp.stack(vp_l), jnp.stack(lfp_l),
            jnp.stack(sp_l), jnp.stack(ks_l), jnp.stack(vs_l), jnp.stack(lfs_l), jnp.stack(ss_l))
```

```python
import functools

import jax
import jax.numpy as jnp
import numpy as np
from jax import lax
from jax.experimental import pallas as pl
from jax.experimental.pallas import tpu as pltpu

ROPE_BASE = 10000.0
LN_EPS = 1e-5
GN_EPS = 1e-5
RET_CHUNK = 128

LANES = 128
VMEM_LIMIT_BYTES = 56 * 1024 * 1024

BF16 = jnp.bfloat16
F32 = jnp.float32
NEG_INF = float("-inf")
LOG2E = 1.4426950408889634


def _params(semantics):
    return pltpu.CompilerParams(dimension_semantics=semantics, vmem_limit_bytes=VMEM_LIMIT_BYTES)


def _dot(a, b):
    return jnp.dot(a, b, preferred_element_type=F32)


def _dot_nt(a, b):
    return lax.dot_general(a, b, (((1,), (1,)), ((), ())), preferred_element_type=F32)


def _lane_cumsum(x):
    lane = lax.broadcasted_iota(jnp.int32, x.shape, 1)
    shift = 1
    while shift < LANES:
        x = x + jnp.where(lane >= shift, pltpu.roll(x, shift, axis=1), 0.0)
        shift *= 2
    return x


def _silu(x):
    return x / (1.0 + jnp.exp(-x))


def _layer_norm(x, g, b):
    mu = jnp.mean(x, axis=-1, keepdims=True)
    xc = x - mu
    var = jnp.mean(xc * xc, axis=-1, keepdims=True)
    return xc * lax.rsqrt(var + LN_EPS) * g + b


def _proj_kernel(x_ref, wfox_ref, wfg_ref, wret_ref, bfg_ref, cos_ref, sin_ref, *refs,
                 fox_w, fox_heads, ret_qk_w, q_scale, k_scale, prompt):
    if prompt:
        (qt_ref, kt_ref, vt_ref, kx_ref, vtb_ref, lft_ref,
         qr_ref, kr_ref, vr_ref, g_ref, carry_ref) = refs
    else:
        qf_ref, kf_ref, vf_ref, lf_ref, qr_ref, kr_ref, vr_ref, g_ref = refs
    xb = x_ref[...].astype(BF16)
    tm = xb.shape[0]

    pf = _dot(xb, wfox_ref[...])
    kf = pf[:, fox_w:2 * fox_w]
    vf = pf[:, 2 * fox_w:3 * fox_w]
    z = _dot(xb, wfg_ref[...]) + bfg_ref[...]
    lf = jnp.minimum(z, 0.0) - jnp.log1p(jnp.exp(-jnp.abs(z)))
    if prompt:
        qt_ref[...] = (pf[:, :fox_w] * (q_scale * LOG2E)).T.astype(BF16)
        kt_ref[...] = kf.T
        vt = vf.T
        vt_ref[...] = vt
        vtb_ref[...] = vt.astype(BF16)

        @pl.when(pl.program_id(1) == 0)
        def _():
            carry_ref[...] = jnp.zeros_like(carry_ref)

        lft = lf.T[:fox_heads, :]
        lft_ref[...] = lft
        off = carry_ref[...]
        blocks = []
        for v in range(tm // LANES):
            blk = _lane_cumsum(lft[:, v * LANES:(v + 1) * LANES]) + off
            blocks.append(blk)
            off = jnp.broadcast_to(blk[:, LANES - 1:LANES], blk.shape)
        carry_ref[...] = off
        ct = jnp.concatenate(blocks, axis=1) * LOG2E
        c = jnp.concatenate([ct, jnp.zeros((LANES - fox_heads, tm), F32)], axis=0).T
        hi = c.astype(BF16).astype(F32)
        r1 = c - hi
        mid = r1.astype(BF16).astype(F32)
        lo = (r1 - mid).astype(BF16).astype(F32)
        ext = hi + pltpu.roll(mid, fox_heads, axis=1) + pltpu.roll(lo, 2 * fox_heads, axis=1)
        kx_ref[:, :fox_w] = kf.astype(BF16)
        kx_ref[:, fox_w:] = ext.astype(BF16)
    else:
        qf_ref[...] = pf[:, :fox_w] * q_scale
        kf_ref[...] = kf
        vf_ref[...] = vf
        lf_ref[...] = lf[:, :fox_heads]

    pr = _dot(xb, wret_ref[...])
    cos = cos_ref[...]
    sin = sin_ref[...]
    lane = lax.broadcasted_iota(jnp.int32, (tm, LANES), 1)
    first_half = (lane & 32) == 0

    def rope(xh):
        partner = jnp.where(first_half, pltpu.roll(xh, LANES - 32, axis=1), pltpu.roll(xh, 32, axis=1))
        return xh * cos + partner * sin

    for j in range(ret_qk_w // LANES):
        sl = slice(j * LANES, (j + 1) * LANES)
        qr_ref[:, sl] = rope(pr[:, sl]).astype(BF16)
        kr_ref[:, sl] = rope(pr[:, ret_qk_w + j * LANES:ret_qk_w + (j + 1) * LANES]) * k_scale
    v_w = (pr.shape[1] - 2 * ret_qk_w) // 2
    vr_ref[...] = pr[:, 2 * ret_qk_w:2 * ret_qk_w + v_w].astype(BF16)
    g_ref[...] = pr[:, 2 * ret_qk_w + v_w:]


def _projection(x, wfox, wfg, wret, bfg, cos_tab, sin_tab, *, fox_heads, fox_dh, ret_heads, ret_dk, prompt, tm):
    b, l, d = x.shape
    fox_w = fox_heads * fox_dh
    ret_qk_w = ret_heads * ret_dk
    ret_v_w = (wret.shape[1] - 2 * ret_qk_w) // 2
    tm = min(tm, l)
    grid = (b, l // tm)

    def row_spec(w):
        return pl.BlockSpec((None, tm, w), lambda i, j: (i, j, 0))

    def col_spec(w):
        return pl.BlockSpec((None, w, tm), lambda i, j: (i, 0, j))

    def const_spec(shape):
        return pl.BlockSpec(shape, lambda i, j: (0,) * len(shape))

    in_specs = [row_spec(d), const_spec(wfox.shape), const_spec(wfg.shape), const_spec(wret.shape),
                const_spec(bfg.shape),
                pl.BlockSpec((tm, LANES), lambda i, j: (j, 0)),
                pl.BlockSpec((tm, LANES), lambda i, j: (j, 0))]

    def rows(w, dt):
        return jax.ShapeDtypeStruct((b, l, w), dt)

    def cols(w, dt):
        return jax.ShapeDtypeStruct((b, w, l), dt)

    ret_shapes = [rows(ret_qk_w, BF16), rows(ret_qk_w, F32), rows(ret_v_w, BF16), rows(ret_v_w, F32)]
    ret_specs = [row_spec(ret_qk_w), row_spec(ret_qk_w), row_spec(ret_v_w), row_spec(ret_v_w)]
    if prompt:
        out_shape = [cols(fox_w, BF16), cols(fox_w, F32), cols(fox_w, F32), rows(fox_w + LANES, BF16),
                     cols(fox_w, BF16), cols(fox_heads, F32)] + ret_shapes
        out_specs = [col_spec(fox_w), col_spec(fox_w), col_spec(fox_w), row_spec(fox_w + LANES),
                     col_spec(fox_w), col_spec(fox_heads)] + ret_specs
        scratch = [pltpu.VMEM((fox_heads, LANES), F32)]
    else:
        out_shape = [rows(fox_w, F32), rows(fox_w, F32), rows(fox_w, F32), rows(fox_heads, F32)] + ret_shapes
        out_specs = [row_spec(fox_w)] * 3 + [row_spec(fox_heads)] + ret_specs
        scratch = []
    kern = functools.partial(_proj_kernel, fox_w=fox_w, fox_heads=fox_heads, ret_qk_w=ret_qk_w,
                             q_scale=fox_dh ** -0.5, k_scale=ret_dk ** -0.5, prompt=prompt)
    return pl.pallas_call(
        kern, grid=grid, in_specs=in_specs, out_specs=out_specs, out_shape=out_shape,
        scratch_shapes=scratch, compiler_params=_params(("parallel", "arbitrary")),
        name="projection_prompt" if prompt else "projection_sample",
    )(x, wfox, wfg, wret, bfg, cos_tab, sin_tab)


def _fox_prompt_kernel(qt_ref, k_ref, kx_ref, vt_ref, o_ref, qx_ref, s0_ref, s1_ref, m_ref, l_ref, acc_ref,
                       *, heads, dh, tq, tk):
    pair = pl.program_id(1)
    qi = pl.program_id(2)
    qt = qt_ref[...]
    row = lax.broadcasted_iota(jnp.int32, qt.shape, 0)
    zero = jnp.zeros_like(qt)
    qx = []
    for h in range(2):
        top = jnp.where((row >= h * dh) & (row < (h + 1) * dh), qt, zero)
        hh = 2 * pair + h
        bias_rows = (row == hh) | (row == hh + heads) | (row == hh + 2 * heads)
        qx.append(jnp.concatenate([top, jnp.where(bias_rows, -1.0, 0.0).astype(qt.dtype)], axis=0))

    for h in range(2):
        qx_ref[h] = qx[h]
    m_ref[...] = jnp.full_like(m_ref, NEG_INF)
    l_ref[...] = jnp.zeros_like(l_ref)
    acc_ref[...] = jnp.zeros_like(acc_ref)

    def scores(kj, dst):
        off = pl.multiple_of(kj * tk, tk)
        kx = jnp.concatenate([k_ref[pl.ds(off, tk), :], kx_ref[pl.ds(off, tk), :]], axis=1)
        for h in range(2):
            dst[h] = _dot(kx, qx_ref[h])

    def absorb(kj, src, diag):
        off = pl.multiple_of(kj * tk, tk)
        vt = vt_ref[:, pl.ds(off, tk)]
        for h in range(2):
            s = src[h]
            if diag is not None:
                key = lax.broadcasted_iota(jnp.int32, s.shape, 0)
                qry = lax.broadcasted_iota(jnp.int32, s.shape, 1)
                s = jnp.where(key + diag * tk <= qry, s, NEG_INF)
            m_old = m_ref[h]
            m_new = jnp.maximum(m_old, jnp.max(s, axis=0, keepdims=True))
            alpha = jnp.exp2(m_old - m_new)
            p = jnp.exp2(s - m_new)
            l_ref[h] = alpha * l_ref[h] + jnp.sum(p, axis=0, keepdims=True)
            acc_ref[h] = alpha * acc_ref[h] + _dot(vt[h * dh:(h + 1) * dh, :], p.astype(BF16))
            m_ref[h] = m_new

    scores(0, s0_ref)

    def body(i, carry):
        kj = 2 * i
        scores(kj + 1, s1_ref)
        absorb(kj, s0_ref, None)
        scores(kj + 2, s0_ref)
        absorb(kj + 1, s1_ref, None)
        return carry

    lax.fori_loop(0, qi, body, 0)
    scores(2 * qi + 1, s1_ref)
    absorb(2 * qi, s0_ref, 0)
    absorb(2 * qi + 1, s1_ref, 1)
    ot = jnp.concatenate([acc_ref[0] / l_ref[0], acc_ref[1] / l_ref[1]], axis=0)
    o_ref[...] = ot.T.astype(o_ref.dtype)


def _fox_prompt(qt, kx, vt, *, heads, dh, tq):
    b, w, l = qt.shape
    assert 2 * dh == LANES and heads % 2 == 0 and 3 * heads <= LANES
    tq = min(tq, l)
    tk = tq // 2
    grid = (b, heads // 2, l // tq)
    return pl.pallas_call(
        functools.partial(_fox_prompt_kernel, heads=heads, dh=dh, tq=tq, tk=tk),
        grid=grid,
        in_specs=[pl.BlockSpec((None, LANES, tq), lambda i, p, j: (i, p, j)),
                  pl.BlockSpec((None, l, LANES), lambda i, p, j: (i, 0, p)),
                  pl.BlockSpec((None, l, LANES), lambda i, p, j: (i, 0, w // LANES)),
                  pl.BlockSpec((None, LANES, l), lambda i, p, j: (i, p, 0))],
        out_specs=pl.BlockSpec((None, tq, LANES), lambda i, p, j: (i, j, p)),
        out_shape=jax.ShapeDtypeStruct((b, l, w), BF16),
        scratch_shapes=[pltpu.VMEM((2, 2 * LANES, tq), BF16),
                        pltpu.VMEM((2, tk, tq), F32), pltpu.VMEM((2, tk, tq), F32),
                        pltpu.VMEM((2, 1, tq), F32), pltpu.VMEM((2, 1, tq), F32),
                        pltpu.VMEM((2, dh, tq), F32)],
        compiler_params=_params(("parallel", "parallel", "arbitrary")),
        name="fox_prompt_attention",
    )(qt, kx, kx, vt)


def _group_norm_gate(o, gate, gn_g, gn_b):
    mu = jnp.mean(o, axis=-1, keepdims=True)
    oc = o - mu
    var = jnp.mean(oc * oc, axis=-1, keepdims=True)
    y = oc * lax.rsqrt(var + GN_EPS) * gn_g + gn_b
    return _silu(gate) * y


def _ret_prompt_kernel(q_ref, k_ref, v_ref, g_ref, dmat_ref, xi_ref, zeta_ref, gng_ref, gnb_ref,
                       y_ref, s_out_ref, s_ref, *, heads, dk, dv, chunk, gc):
    @pl.when(pl.program_id(1) == 0)
    def _():
        s_ref[...] = jnp.zeros_like(s_ref)

    rows = q_ref.shape[0]
    lane = lax.broadcasted_iota(jnp.int32, (chunk, heads * dk), 1)
    for c in range(rows // chunk):
        r = slice(c * chunk, (c + 1) * chunk)
        q = q_ref[r, :]
        k = k_ref[r, :]
        kb = k.astype(BF16)
        kzt = (k * zeta_ref[...]).T.astype(BF16)
        s_old = s_ref[...]
        s_old_b = s_old.astype(BF16)
        for h in range(heads):
            qh = jnp.where((lane >= h * dk) & (lane < (h + 1) * dk), q, jnp.zeros_like(q))
            vh = v_ref[r, h * dv:(h + 1) * dv]
            inner = _dot_nt(qh, kb) * dmat_ref[h]
            o = _dot(inner.astype(BF16), vh) + _dot(qh, s_old_b) * xi_ref[h]
            hs = slice(h * dk, (h + 1) * dk)
            s_ref[hs, :] = gc[h] * s_old[hs, :] + _dot(kzt[hs, :], vh)
            cs = slice(h * dv, (h + 1) * dv)
            y = _group_norm_gate(o, g_ref[r, cs], gng_ref[:, cs], gnb_ref[:, cs])
            y_ref[r, cs] = y.astype(y_ref.dtype)
    s_out_ref[...] = s_ref[...]


def _ret_prompt(qr, kr, vr, g, dmat, xi, zeta, gn_g, gn_b, *, heads, dk, dv, gc, tr):
    b, l, _ = qr.shape
    chunk = dmat.shape[-1]
    tr = min(tr, l)
    grid = (b, l // tr)

    def row_spec(w):
        return pl.BlockSpec((None, tr, w), lambda i, j: (i, j, 0))

    def const_spec(shape):
        return pl.BlockSpec(shape, lambda i, j: (0,) * len(shape))

    return pl.pallas_call(
        functools.partial(_ret_prompt_kernel, heads=heads, dk=dk, dv=dv, chunk=chunk, gc=gc),
        grid=grid,
        in_specs=[row_spec(heads * dk), row_spec(heads * dk), row_spec(heads * dv), row_spec(heads * dv),
                  const_spec(dmat.shape), const_spec(xi.shape), const_spec(zeta.shape),
                  const_spec(gn_g.shape), const_spec(gn_b.shape)],
        out_specs=[row_spec(heads * dv), pl.BlockSpec((None, heads * dk, dv), lambda i, j: (i, 0, 0))],
        out_shape=[jax.ShapeDtypeStruct((b, l, heads * dv), BF16),
                   jax.ShapeDtypeStruct((b, heads * dk, dv), F32)],
        scratch_shapes=[pltpu.VMEM((heads * dk, dv), F32)],
        compiler_params=_params(("parallel", "arbitrary")),
        name="retention_prompt",
    )(qr, kr, vr, g, dmat, xi, zeta, gn_g, gn_b)


def _ret_sample_kernel(q_ref, k_ref, v_ref, g_ref, st_ref, dm_ref, xi_ref, zeta_ref, gc_ref, gng_ref, gnb_ref,
                       y_ref, st_out_ref, *, heads, dk, dv, seqs, t):
    rows = seqs * t
    q = q_ref[...]
    k = k_ref[...]
    kz = k * zeta_ref[...]
    kb = k.astype(BF16)
    lane = lax.broadcasted_iota(jnp.int32, q.shape, 1)

    def head_mask(h):
        return (lane >= h * dk) & (lane < (h + 1) * dk)

    q_all = jnp.concatenate([jnp.where(head_mask(h), q, jnp.zeros_like(q)) for h in range(heads)], axis=0)
    k_all = jnp.concatenate([jnp.where(head_mask(h), kb, jnp.zeros_like(kb)) for h in range(heads)], axis=0)
    kz_all = jnp.concatenate([jnp.where(head_mask(h), kz, jnp.zeros_like(kz)) for h in range(heads)], axis=0)
    v_all = jnp.concatenate([v_ref[:, h * dv:(h + 1) * dv] for h in range(heads)], axis=0)

    inner = _dot_nt(q_all, k_all) * dm_ref[...]
    o = _dot(inner.astype(BF16), v_all)
    kzt = kz_all.T
    n = heads * rows
    rid = lax.broadcasted_iota(jnp.int32, (n, dv), 0)
    cid = lax.broadcasted_iota(jnp.int32, kzt.shape, 1)

    def of_seq(idx, s):
        hit = (idx >= s * t) & (idx < (s + 1) * t)
        for h in range(1, heads):
            hit = hit | ((idx >= h * rows + s * t) & (idx < h * rows + (s + 1) * t))
        return hit

    o_state = jnp.zeros((n, dv), F32)
    for s in range(seqs):
        s0 = st_ref[s]
        o_state = o_state + jnp.where(of_seq(rid, s), _dot(q_all, s0.astype(BF16)), 0.0)
        ds = _dot(jnp.where(of_seq(cid, s), kzt, 0.0).astype(BF16), v_all)
        st_out_ref[s] = gc_ref[...] * s0 + ds
    o = o + o_state * xi_ref[...]

    mu = jnp.mean(o, axis=-1, keepdims=True)
    oc = o - mu
    var = jnp.mean(oc * oc, axis=-1, keepdims=True)
    yn = oc * lax.rsqrt(var + GN_EPS)
    for h in range(heads):
        cs = slice(h * dv, (h + 1) * dv)
        y = yn[h * rows:(h + 1) * rows, :] * gng_ref[:, cs] + gnb_ref[:, cs]
        y_ref[:, cs] = (_silu(g_ref[:, cs]) * y).astype(y_ref.dtype)


def _ret_sample(qr, kr, vr, g, state, dm, xi, zeta, gc, gn_g, gn_b, *, heads, dk, dv, t, seqs):
    n_rows = qr.shape[0]
    n_seq = n_rows // t
    rows = seqs * t
    grid = (n_seq // seqs,)

    def row_spec(w):
        return pl.BlockSpec((rows, w), lambda i: (i, 0))

    def const_spec(shape):
        return pl.BlockSpec(shape, lambda i: (0,) * len(shape))

    st_spec = pl.BlockSpec((seqs, heads * dk, dv), lambda i: (i, 0, 0))
    return pl.pallas_call(
        functools.partial(_ret_sample_kernel, heads=heads, dk=dk, dv=dv, seqs=seqs, t=t),
        grid=grid,
        in_specs=[row_spec(heads * dk), row_spec(heads * dk), row_spec(heads * dv), row_spec(heads * dv),
                  st_spec, const_spec(dm.shape), const_spec(xi.shape), const_spec(zeta.shape),
                  const_spec(gc.shape), const_spec(gn_g.shape), const_spec(gn_b.shape)],
        out_specs=[row_spec(heads * dv), st_spec],
        out_shape=[jax.ShapeDtypeStruct((n_rows, heads * dv), BF16),
                   jax.ShapeDtypeStruct(state.shape, F32)],
        compiler_params=_params(("parallel",)),
        name="retention_sample",
    )(qr, kr, vr, g, state, dm, xi, zeta, gc, gn_g, gn_b)


def _fox_sample_kernel(pt_ref, q_ref, kn_ref, vn_ref, lfn_ref, kt_hbm, vt_hbm, lf_hbm, o_ref,
                       kbuf, vbuf, lfbuf, sem, qbd_ref, qbdb_ref, m_ref, l_ref, acc_ref, carry_ref,
                       *, pps, heads, dh, t, page):
    b = pl.program_id(0)
    j = pl.program_id(1)
    n_steps = pl.num_programs(1)
    step = b * n_steps + j
    slot = lax.rem(step, 2)
    w = heads * dh

    def page_copies(seq, group, buf_slot):
        copies = []
        for i in range(pps):
            pid = pt_ref[seq, group * pps + i]
            copies.append(pltpu.make_async_copy(kt_hbm.at[pid], kbuf.at[buf_slot, i], sem.at[buf_slot, 0]))
            copies.append(pltpu.make_async_copy(vt_hbm.at[pid], vbuf.at[buf_slot, i], sem.at[buf_slot, 1]))
            copies.append(pltpu.make_async_copy(lf_hbm.at[pid], lfbuf.at[buf_slot, i], sem.at[buf_slot, 2]))
        return copies

    @pl.when(step == 0)
    def _():
        for c in page_copies(0, 0, 0):
            c.start()

    @pl.when(step + 1 < pl.num_programs(0) * n_steps)
    def _():
        wrap = j == n_steps - 1
        for c in page_copies(jnp.where(wrap, b + 1, b), jnp.where(wrap, 0, j + 1), 1 - slot):
            c.start()

    for c in page_copies(b, j, slot):
        c.wait()
    kt_refs = [kbuf.at[slot, i] for i in range(pps)]
    vt_refs = [vbuf.at[slot, i] for i in range(pps)]
    lf_refs = [lfbuf.at[slot, i] for i in range(pps)]
    head_rows = lax.broadcasted_iota(jnp.int32, (heads, w), 0)
    head_lanes = lax.broadcasted_iota(jnp.int32, (heads, w), 1)
    own = (head_lanes >= head_rows * dh) & (head_lanes < (head_rows + 1) * dh)

    def tile_tokens(a):
        return jnp.concatenate([a] * t, axis=0)

    @pl.when(j == 0)
    def _():
        q = q_ref[...]
        qbd = jnp.concatenate(
            [jnp.where(own, jnp.broadcast_to(q[i:i + 1, :], (heads, w)), 0.0) for i in range(t)], axis=0)
        qbd_ref[...] = qbd
        qbdb_ref[...] = qbd.astype(BF16)
        m_ref[...] = jnp.full_like(m_ref, NEG_INF)
        l_ref[...] = jnp.zeros_like(l_ref)
        acc_ref[...] = jnp.zeros_like(acc_ref)
        carry_ref[...] = jnp.zeros_like(carry_ref)

    qbdb = qbdb_ref[...]
    carry = carry_ref[...]
    s_parts = []
    for i in range(pps):
        wc = _lane_cumsum(lf_refs[i][...])
        s_parts.append(_dot(qbdb, kt_refs[i][...].astype(BF16)) - tile_tokens(carry + wc))
        carry = carry + jnp.broadcast_to(wc[:, page - 1:page], wc.shape)
    carry_ref[...] = carry
    s = jnp.concatenate(s_parts, axis=1)
    m_old = m_ref[...]
    m_new = jnp.maximum(m_old, jnp.max(s, axis=1, keepdims=True))
    alpha = jnp.exp(m_old - m_new)
    p = jnp.exp(s - m_new)
    vt = jnp.concatenate([vt_refs[i][...].astype(BF16) for i in range(pps)], axis=1)
    l_ref[...] = alpha * l_ref[...] + jnp.sum(p, axis=1, keepdims=True)
    acc_ref[...] = alpha * acc_ref[...] + _dot_nt(p.astype(BF16), vt)
    m_ref[...] = m_new

    @pl.when(j == pl.num_programs(1) - 1)
    def _():
        qbd = qbd_ref[...]
        kn = kn_ref[...]
        vn = vn_ref[...]
        bias = carry_ref[...] + _lane_cumsum(lfn_ref[...])
        row = lax.broadcasted_iota(jnp.int32, (t * heads, 1), 0)
        cols = []
        for u in range(t):
            s_u = jnp.sum(qbd * kn[u:u + 1, :], axis=1, keepdims=True) - tile_tokens(bias[:, u:u + 1])
            cols.append(jnp.where(row >= u * heads, s_u, NEG_INF))
        m_old = m_ref[...]
        m_new = m_old
        for s_u in cols:
            m_new = jnp.maximum(m_new, s_u)
        alpha = jnp.exp(m_old - m_new)
        l_new = alpha * l_ref[...]
        acc = alpha * acc_ref[...]
        for u, s_u in enumerate(cols):
            p_u = jnp.exp(s_u - m_new)
            l_new = l_new + p_u
            acc = acc + p_u * vn[u:u + 1, :]
        o = acc / l_new
        o = jnp.where(own[None], o.reshape(t, heads, w), 0.0)
        o_ref[...] = jnp.sum(o, axis=1).astype(o_ref.dtype)


def _fox_sample(page_table, q, k_new, v_new, lfn_pad, cache_kt, cache_vt, cache_lft, *, pps):
    n_seq, t, w = q.shape
    n_pages = page_table.shape[1]
    _, heads, page = cache_lft.shape
    dh = w // heads
    assert page == LANES and n_pages % pps == 0
    grid = (n_seq, n_pages // pps)

    def seq_spec(shape):
        return pl.BlockSpec((None,) + shape, lambda b, j, pt: (b, 0, 0))

    hbm = pl.BlockSpec(memory_space=pl.ANY)
    in_specs = [seq_spec((t, w)), seq_spec((t, w)), seq_spec((t, w)), seq_spec((heads, page)), hbm, hbm, hbm]
    grid_spec = pltpu.PrefetchScalarGridSpec(
        num_scalar_prefetch=1, grid=grid, in_specs=in_specs,
        out_specs=pl.BlockSpec((None, t, w), lambda b, j, pt: (b, 0, 0)),
        scratch_shapes=[pltpu.VMEM((2, pps, w, page), F32), pltpu.VMEM((2, pps, w, page), F32),
                        pltpu.VMEM((2, pps, heads, page), F32), pltpu.SemaphoreType.DMA((2, 3)),
                        pltpu.VMEM((t * heads, w), F32), pltpu.VMEM((t * heads, w), BF16),
                        pltpu.VMEM((t * heads, 1), F32), pltpu.VMEM((t * heads, 1), F32),
                        pltpu.VMEM((t * heads, w), F32), pltpu.VMEM((heads, page), F32)])
    return pl.pallas_call(
        functools.partial(_fox_sample_kernel, pps=pps, heads=heads, dh=dh, t=t, page=page),
        grid_spec=grid_spec,
        out_shape=jax.ShapeDtypeStruct((n_seq, t, w), BF16),
        compiler_params=_params(("arbitrary", "arbitrary")),
        name="fox_sample_attention",
    )(page_table, q, k_new, v_new, lfn_pad, cache_kt, cache_vt, cache_lft)


def _finish_kernel(x_ref, of_ref, yr_ref, wof_ref, wor_ref, g1_ref, b1_ref, wg_ref, wu_ref, wd_ref,
                   g2_ref, b2_ref, y_ref, *, alpha):
    mix = _dot(of_ref[...], wof_ref[...]) + _dot(yr_ref[...], wor_ref[...])
    h = _layer_norm(alpha * x_ref[...] + mix, g1_ref[...], b1_ref[...])
    hb = h.astype(BF16)
    act = (_silu(_dot(hb, wg_ref[...])) * _dot(hb, wu_ref[...])).astype(BF16)
    f = _dot(act, wd_ref[...])
    y_ref[...] = _layer_norm(alpha * h + f, g2_ref[...], b2_ref[...])


def _finish(x, o_fox, yr, wof, wor, ln1_g, ln1_b, w_gate, w_up, w_down, ln2_g, ln2_b, *, alpha, tm):
    n, d = x.shape
    tm = min(tm, n)

    def row_spec(w):
        return pl.BlockSpec((tm, w), lambda i: (i, 0))

    def const_spec(a):
        return pl.BlockSpec(a.shape, lambda i: (0,) * a.ndim, pipeline_mode=pl.Buffered(1))

    consts = (wof, wor, ln1_g, ln1_b, w_gate, w_up, w_down, ln2_g, ln2_b)
    return pl.pallas_call(
        functools.partial(_finish_kernel, alpha=alpha),
        grid=(n // tm,),
        in_specs=[row_spec(d), row_spec(o_fox.shape[1]), row_spec(yr.shape[1])] + [const_spec(a) for a in consts],
        out_specs=row_spec(d),
        out_shape=jax.ShapeDtypeStruct((n, d), F32),
        compiler_params=_params(("parallel",)),
        name="finish",
    )(x, o_fox, yr, *consts)


def _rope_tables(pos, dk):
    half = dk // 2
    inv = ROPE_BASE ** (-jnp.arange(half, dtype=F32) / half)
    ang = pos.astype(F32)[:, None] * inv[None, :]
    cos = jnp.cos(ang)
    sin = jnp.sin(ang)
    reps = LANES // dk
    cos_row = jnp.tile(jnp.concatenate([cos, cos], axis=1), (1, reps))
    sin_row = jnp.tile(jnp.concatenate([-sin, sin], axis=1), (1, reps))
    return cos_row, sin_row


def _decay_tables(heads, chunk):
    f = np.float32
    log_g = np.log1p(-(f(2.0) ** (f(-5.0) - np.arange(heads, dtype=f)))).astype(f)
    idx = np.arange(chunk, dtype=f)
    diff = idx[:, None] - idx[None, :]
    dmat = np.where(diff[None] >= 0, np.exp(np.maximum(diff, f(0.0))[None] * log_g[:, None, None]), f(0.0)).astype(f)
    xi = np.exp((idx + f(1.0))[:, None] * log_g[None, :]).astype(f)
    zeta = np.exp((f(chunk) - f(1.0) - idx)[:, None] * log_g[None, :]).astype(f)
    gc = np.exp(f(chunk) * log_g).astype(f)
    return dmat, xi, zeta, gc


def kernel(x_prompt, x_sample, cache_k, cache_v, cache_logf, state_ret, page_table, w_in, b_fgate, ret_gn_g,
           ret_gn_b, w_o, ln1_g, ln1_b, w_gate, w_up, w_down, ln2_g, ln2_b):
    depth, d_model, _ = w_in.shape
    batch, seq, _ = x_prompt.shape
    n_seq, t_dec, _ = x_sample.shape
    _, n_pool, page, fox_heads, fox_dh = cache_k.shape
    _, _, ret_heads, ret_dk, ret_dv = state_ret.shape
    fox_w = fox_heads * fox_dh
    ret_qk_w = ret_heads * ret_dk
    ret_v_w = ret_heads * ret_dv
    past_len = page_table.shape[1] * page
    alpha = (2 * depth) ** 0.25
    o_ff = 3 * fox_w
    o_rq = o_ff + fox_heads
    assert LANES % ret_dk == 0 and ret_qk_w % LANES == 0 and seq % RET_CHUNK == 0 and t_dec < RET_CHUNK

    cos_p, sin_p = _rope_tables(jnp.arange(seq, dtype=jnp.int32), ret_dk)
    pos_s = past_len + jnp.arange(t_dec, dtype=jnp.int32)
    cos_s, sin_s = _rope_tables(jnp.tile(pos_s, n_seq), ret_dk)
    dmat_p, xi_p, zeta_p, gc_p = _decay_tables(ret_heads, RET_CHUNK)
    xi_p_tab = np.ascontiguousarray(np.broadcast_to(xi_p.T[:, :, None], (ret_heads, RET_CHUNK, ret_dv)))
    zeta_p_tab = np.repeat(zeta_p, ret_dk, axis=1)
    gc_p_static = tuple(float(v) for v in gc_p)

    seqs = 8
    while n_seq % seqs:
        seqs //= 2
    rows = seqs * t_dec
    dmat_s, xi_s, zeta_s, gc_s = _decay_tables(ret_heads, t_dec)
    r_h = np.repeat(np.arange(ret_heads), rows)
    r_s = np.tile(np.repeat(np.arange(seqs), t_dec), ret_heads)
    r_t = np.tile(np.arange(t_dec), ret_heads * seqs)
    dm_s_tab = np.where((r_h[:, None] == r_h[None, :]) & (r_s[:, None] == r_s[None, :]),
                        dmat_s[r_h[:, None], r_t[:, None], r_t[None, :]], np.float32(0.0)).astype(np.float32)
    xi_s_tab = np.ascontiguousarray(np.broadcast_to(xi_s[r_t, r_h][:, None], (ret_heads * rows, ret_dv)))
    zeta_s_tab = np.tile(np.repeat(zeta_s, ret_dk, axis=1), (seqs, 1))
    gc_s_tab = np.ascontiguousarray(np.broadcast_to(np.repeat(gc_s, ret_dk)[:, None], (ret_qk_w, ret_dv)))

    pps = 16
    while page_table.shape[1] % pps:
        pps //= 2

    hp = x_prompt
    hs = x_sample.reshape(1, n_seq * t_dec, d_model)
    kp_l, vp_l, lfp_l, sp_l, ks_l, vs_l, lfs_l, ss_l = [], [], [], [], [], [], [], []
    for i in range(depth):
        w = w_in[i]
        wfox = w[:, :o_ff].astype(BF16)
        wfg = jnp.pad(w[:, o_ff:o_rq], ((0, 0), (0, LANES - fox_heads))).astype(BF16)
        wret = w[:, o_rq:].astype(BF16)
        bfg = jnp.pad(b_fgate[i], (0, LANES - fox_heads))[None, :]
        wof = w_o[i][:fox_w].astype(BF16)
        wor = w_o[i][fox_w:].astype(BF16)
        gn_g = ret_gn_g[i][None, :]
        gn_b = ret_gn_b[i][None, :]
        tail = (wof, wor, ln1_g[i][None, :], ln1_b[i][None, :], w_gate[i].astype(BF16), w_up[i].astype(BF16),
                w_down[i].astype(BF16), ln2_g[i][None, :], ln2_b[i][None, :])
        proj = functools.partial(_projection, fox_heads=fox_heads, fox_dh=fox_dh, ret_heads=ret_heads,
                                 ret_dk=ret_dk, tm=512)

        qtb, kt, vt, kx, vtb, lft, qr, kr, vr, g = proj(hp, wfox, wfg, wret, bfg, cos_p, sin_p, prompt=True)
        o_f = _fox_prompt(qtb, kx, vtb, heads=fox_heads, dh=fox_dh, tq=512)
        yr, s_p = _ret_prompt(qr, kr, vr, g, dmat_p, xi_p_tab, zeta_p_tab, gn_g, gn_b, heads=ret_heads,
                              dk=ret_dk, dv=ret_dv, gc=gc_p_static, tr=512)
        kp_l.append(jnp.transpose(kt.reshape(batch, fox_heads, fox_dh, seq), (0, 3, 1, 2)))
        vp_l.append(jnp.transpose(vt.reshape(batch, fox_heads, fox_dh, seq), (0, 3, 1, 2)))
        lfp_l.append(jnp.transpose(lft, (0, 2, 1)))
        sp_l.append(s_p.reshape(batch, ret_heads, ret_dk, ret_dv).astype(state_ret.dtype))
        new_hp = _finish(hp.reshape(batch * seq, d_model), o_f.reshape(batch * seq, fox_w),
                         yr.reshape(batch * seq, ret_v_w), *tail, alpha=alpha, tm=512)

        qf, kf, vf, lf, qr, kr, vr, g = proj(hs, wfox, wfg, wret, bfg, cos_s, sin_s, prompt=False)
        cache_kt = jnp.transpose(cache_k[i], (0, 2, 3, 1)).reshape(n_pool, fox_w, page)
        cache_vt = jnp.transpose(cache_v[i], (0, 2, 3, 1)).reshape(n_pool, fox_w, page)
        cache_lft = jnp.transpose(cache_logf[i], (0, 2, 1))
        lfn = jnp.swapaxes(lf.reshape(n_seq, t_dec, fox_heads), 1, 2)
        lfn_pad = jnp.pad(lfn, ((0, 0), (0, 0), (0, page - t_dec)))
        o_f = _fox_sample(page_table, qf.reshape(n_seq, t_dec, fox_w), kf.reshape(n_seq, t_dec, fox_w),
                          vf.reshape(n_seq, t_dec, fox_w), lfn_pad, cache_kt, cache_vt, cache_lft, pps=pps)
        yr, s_s = _ret_sample(qr[0], kr[0], vr[0], g[0], state_ret[i].reshape(n_seq, ret_qk_w, ret_dv),
                              dm_s_tab, xi_s_tab, zeta_s_tab, gc_s_tab, gn_g, gn_b, heads=ret_heads, dk=ret_dk,
                              dv=ret_dv, t=t_dec, seqs=seqs)
        ks_l.append(kf.reshape(n_seq, t_dec, fox_heads, fox_dh))
        vs_l.append(vf.reshape(n_seq, t_dec, fox_heads, fox_dh))
        lfs_l.append(lf.reshape(n_seq, t_dec, fox_heads))
        ss_l.append(s_s.reshape(n_seq, ret_heads, ret_dk, ret_dv).astype(state_ret.dtype))
        hs = _finish(hs[0], o_f.reshape(n_seq * t_dec, fox_w), yr, *tail, alpha=alpha,
                     tm=512).reshape(1, n_seq * t_dec, d_model)
        hp = new_hp.reshape(batch, seq, d_model)
    return (hp, hs.reshape(n_seq, t_dec, d_model), jnp.stack(kp_l), jnp.stack(vp_l), jnp.stack(lfp_l),
            jnp.stack(sp_l), jnp.stack(ks_l), jnp.stack(vs_l), jnp.stack(lfs_l), jnp.stack(ss_l))
```

```python
import functools

import jax
import jax.numpy as jnp
import numpy as np
from jax import lax
from jax.experimental import pallas as pl
from jax.experimental.pallas import tpu as pltpu

ROPE_BASE = 10000.0
LN_EPS = 1e-5
GN_EPS = 1e-5
RET_CHUNK = 128

LANES = 128
ONES_ROWS = 16
VMEM_LIMIT_BYTES = 56 * 1024 * 1024

BF16 = jnp.bfloat16
F32 = jnp.float32
NEG_INF = float("-inf")
LOG2E = 1.4426950408889634


def _params(semantics):
    return pltpu.CompilerParams(dimension_semantics=semantics, vmem_limit_bytes=VMEM_LIMIT_BYTES)


def _dot(a, b):
    return jnp.dot(a, b, preferred_element_type=F32)


def _dot_nt(a, b):
    return lax.dot_general(a, b, (((1,), (1,)), ((), ())), preferred_element_type=F32)


def _lane_cumsum(x):
    lane = lax.broadcasted_iota(jnp.int32, x.shape, 1)
    shift = 1
    while shift < LANES:
        x = x + jnp.where(lane >= shift, pltpu.roll(x, shift, axis=1), 0.0)
        shift *= 2
    return x


def _silu(x):
    return x / (1.0 + jnp.exp(-x))


def _layer_norm(x, g, b):
    mu = jnp.mean(x, axis=-1, keepdims=True)
    xc = x - mu
    var = jnp.mean(xc * xc, axis=-1, keepdims=True)
    return xc * lax.rsqrt(var + LN_EPS) * g + b


def _proj_kernel(x_ref, wfox_ref, wfg_ref, wret_ref, bfg_ref, cos_ref, sin_ref, *refs,
                 fox_w, fox_heads, ret_qk_w, q_scale, k_scale, prompt):
    if prompt:
        (qt_ref, kt_ref, vt_ref, kx_ref, vtb_ref, lft_ref,
         qr_ref, kr_ref, vr_ref, g_ref, carry_ref) = refs
    else:
        qf_ref, kf_ref, vf_ref, lf_ref, qr_ref, kr_ref, vr_ref, g_ref = refs
    xb = x_ref[...].astype(BF16)
    tm = xb.shape[0]

    pf = _dot(xb, wfox_ref[...])
    kf = pf[:, fox_w:2 * fox_w]
    vf = pf[:, 2 * fox_w:3 * fox_w]
    z = _dot(xb, wfg_ref[...]) + bfg_ref[...]
    lf = jnp.minimum(z, 0.0) - jnp.log1p(jnp.exp(-jnp.abs(z)))
    if prompt:
        qt_ref[...] = (pf[:, :fox_w] * (q_scale * LOG2E)).T.astype(BF16)
        kt_ref[...] = kf.T
        vt = vf.T
        vt_ref[...] = vt
        vtb_ref[...] = vt.astype(BF16)

        @pl.when(pl.program_id(1) == 0)
        def _():
            carry_ref[...] = jnp.zeros_like(carry_ref)

        lft = lf.T[:fox_heads, :]
        lft_ref[...] = lft
        off = carry_ref[...]
        blocks = []
        for v in range(tm // LANES):
            blk = _lane_cumsum(lft[:, v * LANES:(v + 1) * LANES]) + off
            blocks.append(blk)
            off = jnp.broadcast_to(blk[:, LANES - 1:LANES], blk.shape)
        carry_ref[...] = off
        ct = jnp.concatenate(blocks, axis=1) * LOG2E
        c = jnp.concatenate([ct, jnp.zeros((LANES - fox_heads, tm), F32)], axis=0).T
        hi = c.astype(BF16).astype(F32)
        r1 = c - hi
        mid = r1.astype(BF16).astype(F32)
        lo = (r1 - mid).astype(BF16).astype(F32)
        ext = hi + pltpu.roll(mid, fox_heads, axis=1) + pltpu.roll(lo, 2 * fox_heads, axis=1)
        kx_ref[:, :fox_w] = kf.astype(BF16)
        kx_ref[:, fox_w:] = ext.astype(BF16)
    else:
        qf_ref[...] = pf[:, :fox_w] * q_scale
        kf_ref[...] = kf
        vf_ref[...] = vf
        lf_ref[...] = lf[:, :fox_heads]

    pr = _dot(xb, wret_ref[...])
    cos = cos_ref[...]
    sin = sin_ref[...]
    lane = lax.broadcasted_iota(jnp.int32, (tm, LANES), 1)
    first_half = (lane & 32) == 0

    def rope(xh):
        partner = jnp.where(first_half, pltpu.roll(xh, LANES - 32, axis=1), pltpu.roll(xh, 32, axis=1))
        return xh * cos + partner * sin

    for j in range(ret_qk_w // LANES):
        sl = slice(j * LANES, (j + 1) * LANES)
        qr_ref[:, sl] = rope(pr[:, sl]).astype(BF16)
        kr_ref[:, sl] = rope(pr[:, ret_qk_w + j * LANES:ret_qk_w + (j + 1) * LANES]) * k_scale
    v_w = (pr.shape[1] - 2 * ret_qk_w) // 2
    vr_ref[...] = pr[:, 2 * ret_qk_w:2 * ret_qk_w + v_w].astype(BF16)
    g_ref[...] = pr[:, 2 * ret_qk_w + v_w:]


def _projection(x, wfox, wfg, wret, bfg, cos_tab, sin_tab, *, fox_heads, fox_dh, ret_heads, ret_dk, prompt, tm):
    b, l, d = x.shape
    fox_w = fox_heads * fox_dh
    ret_qk_w = ret_heads * ret_dk
    ret_v_w = (wret.shape[1] - 2 * ret_qk_w) // 2
    tm = min(tm, l)
    grid = (b, l // tm)

    def row_spec(w):
        return pl.BlockSpec((None, tm, w), lambda i, j: (i, j, 0))

    def col_spec(w):
        return pl.BlockSpec((None, w, tm), lambda i, j: (i, 0, j))

    def const_spec(shape):
        return pl.BlockSpec(shape, lambda i, j: (0,) * len(shape))

    in_specs = [row_spec(d), const_spec(wfox.shape), const_spec(wfg.shape), const_spec(wret.shape),
                const_spec(bfg.shape),
                pl.BlockSpec((tm, LANES), lambda i, j: (j, 0)),
                pl.BlockSpec((tm, LANES), lambda i, j: (j, 0))]

    def rows(w, dt):
        return jax.ShapeDtypeStruct((b, l, w), dt)

    def cols(w, dt):
        return jax.ShapeDtypeStruct((b, w, l), dt)

    ret_shapes = [rows(ret_qk_w, BF16), rows(ret_qk_w, F32), rows(ret_v_w, BF16), rows(ret_v_w, F32)]
    ret_specs = [row_spec(ret_qk_w), row_spec(ret_qk_w), row_spec(ret_v_w), row_spec(ret_v_w)]
    if prompt:
        out_shape = [cols(fox_w, BF16), cols(fox_w, F32), cols(fox_w, F32), rows(fox_w + LANES, BF16),
                     cols(fox_w, BF16), cols(fox_heads, F32)] + ret_shapes
        out_specs = [col_spec(fox_w), col_spec(fox_w), col_spec(fox_w), row_spec(fox_w + LANES),
                     col_spec(fox_w), col_spec(fox_heads)] + ret_specs
        scratch = [pltpu.VMEM((fox_heads, LANES), F32)]
    else:
        out_shape = [rows(fox_w, F32), rows(fox_w, F32), rows(fox_w, F32), rows(fox_heads, F32)] + ret_shapes
        out_specs = [row_spec(fox_w)] * 3 + [row_spec(fox_heads)] + ret_specs
        scratch = []
    kern = functools.partial(_proj_kernel, fox_w=fox_w, fox_heads=fox_heads, ret_qk_w=ret_qk_w,
                             q_scale=fox_dh ** -0.5, k_scale=ret_dk ** -0.5, prompt=prompt)
    return pl.pallas_call(
        kern, grid=grid, in_specs=in_specs, out_specs=out_specs, out_shape=out_shape,
        scratch_shapes=scratch, compiler_params=_params(("parallel", "arbitrary")),
        name="projection_prompt" if prompt else "projection_sample",
    )(x, wfox, wfg, wret, bfg, cos_tab, sin_tab)


def _fox_prompt_kernel(qt_ref, k_ref, kx_ref, vt_ref, o_ref, qx_ref, s0_ref, s1_ref, m_ref, acc_ref,
                       *, heads, dh, tq, tk):
    pair = pl.program_id(1)
    qi = pl.program_id(2)
    qt = qt_ref[...]
    row = lax.broadcasted_iota(jnp.int32, qt.shape, 0)
    zero = jnp.zeros_like(qt)
    qx = []
    for h in range(2):
        top = jnp.where((row >= h * dh) & (row < (h + 1) * dh), qt, zero)
        hh = 2 * pair + h
        bias_rows = (row == hh) | (row == hh + heads) | (row == hh + 2 * heads)
        qx.append(jnp.concatenate([top, jnp.where(bias_rows, -1.0, 0.0).astype(qt.dtype)], axis=0))

    for h in range(2):
        qx_ref[h] = qx[h]
    m_ref[...] = jnp.full_like(m_ref, NEG_INF)
    acc_ref[...] = jnp.zeros_like(acc_ref)
    ones = jnp.ones((ONES_ROWS, tk), qt.dtype)

    def scores(kj, dst):
        off = pl.multiple_of(kj * tk, tk)
        kx = jnp.concatenate([k_ref[pl.ds(off, tk), :], kx_ref[pl.ds(off, tk), :]], axis=1)
        for h in range(2):
            dst[h] = _dot(kx, qx_ref[h])

    def absorb(kj, src, diag):
        off = pl.multiple_of(kj * tk, tk)
        vt = vt_ref[:, pl.ds(off, tk)]
        for h in range(2):
            s = src[h]
            if diag is not None:
                key = lax.broadcasted_iota(jnp.int32, s.shape, 0)
                qry = lax.broadcasted_iota(jnp.int32, s.shape, 1)
                s = jnp.where(key + diag * tk <= qry, s, NEG_INF)
            m_old = m_ref[h]
            m_new = jnp.maximum(m_old, jnp.max(s, axis=0, keepdims=True))
            alpha = jnp.exp2(m_old - m_new)
            p = jnp.exp2(s - m_new)
            vt1 = jnp.concatenate([vt[h * dh:(h + 1) * dh, :], ones], axis=0)
            acc_ref[h] = alpha * acc_ref[h] + _dot(vt1, p.astype(BF16))
            m_ref[h] = m_new

    scores(0, s0_ref)

    def body(i, carry):
        kj = 2 * i
        scores(kj + 1, s1_ref)
        absorb(kj, s0_ref, None)
        scores(kj + 2, s0_ref)
        absorb(kj + 1, s1_ref, None)
        return carry

    lax.fori_loop(0, qi, body, 0)
    scores(2 * qi + 1, s1_ref)
    absorb(2 * qi, s0_ref, 0)
    absorb(2 * qi + 1, s1_ref, 1)
    ot = jnp.concatenate([acc_ref[h, :dh, :] / acc_ref[h, dh:dh + 1, :] for h in range(2)], axis=0)
    o_ref[...] = ot.T.astype(o_ref.dtype)


def _fox_prompt(qt, kx, vt, *, heads, dh, tq):
    b, w, l = qt.shape
    assert 2 * dh == LANES and heads % 2 == 0 and 3 * heads <= LANES
    tq = min(tq, l)
    tk = tq // 2
    grid = (b, heads // 2, l // tq)
    return pl.pallas_call(
        functools.partial(_fox_prompt_kernel, heads=heads, dh=dh, tq=tq, tk=tk),
        grid=grid,
        in_specs=[pl.BlockSpec((None, LANES, tq), lambda i, p, j: (i, p, j)),
                  pl.BlockSpec((None, l, LANES), lambda i, p, j: (i, 0, p)),
                  pl.BlockSpec((None, l, LANES), lambda i, p, j: (i, 0, w // LANES)),
                  pl.BlockSpec((None, LANES, l), lambda i, p, j: (i, p, 0))],
        out_specs=pl.BlockSpec((None, tq, LANES), lambda i, p, j: (i, j, p)),
        out_shape=jax.ShapeDtypeStruct((b, l, w), BF16),
        scratch_shapes=[pltpu.VMEM((2, 2 * LANES, tq), BF16),
                        pltpu.VMEM((2, tk, tq), F32), pltpu.VMEM((2, tk, tq), F32),
                        pltpu.VMEM((2, 1, tq), F32), pltpu.VMEM((2, dh + ONES_ROWS, tq), F32)],
        compiler_params=_params(("parallel", "parallel", "arbitrary")),
        name="fox_prompt_attention",
    )(qt, kx, kx, vt)


def _group_norm_gate(o, gate, gn_g, gn_b):
    mu = jnp.mean(o, axis=-1, keepdims=True)
    oc = o - mu
    var = jnp.mean(oc * oc, axis=-1, keepdims=True)
    y = oc * lax.rsqrt(var + GN_EPS) * gn_g + gn_b
    return _silu(gate) * y


def _ret_prompt_kernel(q_ref, k_ref, v_ref, g_ref, dmat_ref, xi_ref, zeta_ref, gng_ref, gnb_ref,
                       y_ref, s_out_ref, s_ref, *, heads, dk, dv, chunk, gc):
    @pl.when(pl.program_id(1) == 0)
    def _():
        s_ref[...] = jnp.zeros_like(s_ref)

    rows = q_ref.shape[0]
    lane = lax.broadcasted_iota(jnp.int32, (chunk, heads * dk), 1)
    for c in range(rows // chunk):
        r = slice(c * chunk, (c + 1) * chunk)
        q = q_ref[r, :]
        k = k_ref[r, :]
        kb = k.astype(BF16)
        kzt = (k * zeta_ref[...]).T.astype(BF16)
        s_old = s_ref[...]
        s_old_b = s_old.astype(BF16)
        for h in range(heads):
            qh = jnp.where((lane >= h * dk) & (lane < (h + 1) * dk), q, jnp.zeros_like(q))
            vh = v_ref[r, h * dv:(h + 1) * dv]
            inner = _dot_nt(qh, kb) * dmat_ref[h]
            o = _dot(inner.astype(BF16), vh) + _dot(qh, s_old_b) * xi_ref[h]
            hs = slice(h * dk, (h + 1) * dk)
            s_ref[hs, :] = gc[h] * s_old[hs, :] + _dot(kzt[hs, :], vh)
            cs = slice(h * dv, (h + 1) * dv)
            y = _group_norm_gate(o, g_ref[r, cs], gng_ref[:, cs], gnb_ref[:, cs])
            y_ref[r, cs] = y.astype(y_ref.dtype)
    s_out_ref[...] = s_ref[...]


def _ret_prompt(qr, kr, vr, g, dmat, xi, zeta, gn_g, gn_b, *, heads, dk, dv, gc, tr):
    b, l, _ = qr.shape
    chunk = dmat.shape[-1]
    tr = min(tr, l)
    grid = (b, l // tr)

    def row_spec(w):
        return pl.BlockSpec((None, tr, w), lambda i, j: (i, j, 0))

    def const_spec(shape):
        return pl.BlockSpec(shape, lambda i, j: (0,) * len(shape))

    return pl.pallas_call(
        functools.partial(_ret_prompt_kernel, heads=heads, dk=dk, dv=dv, chunk=chunk, gc=gc),
        grid=grid,
        in_specs=[row_spec(heads * dk), row_spec(heads * dk), row_spec(heads * dv), row_spec(heads * dv),
                  const_spec(dmat.shape), const_spec(xi.shape), const_spec(zeta.shape),
                  const_spec(gn_g.shape), const_spec(gn_b.shape)],
        out_specs=[row_spec(heads * dv), pl.BlockSpec((None, heads * dk, dv), lambda i, j: (i, 0, 0))],
        out_shape=[jax.ShapeDtypeStruct((b, l, heads * dv), BF16),
                   jax.ShapeDtypeStruct((b, heads * dk, dv), F32)],
        scratch_shapes=[pltpu.VMEM((heads * dk, dv), F32)],
        compiler_params=_params(("parallel", "arbitrary")),
        name="retention_prompt",
    )(qr, kr, vr, g, dmat, xi, zeta, gn_g, gn_b)


def _ret_sample_kernel(q_ref, k_ref, v_ref, g_ref, st_ref, dm_ref, xi_ref, zeta_ref, gc_ref, gng_ref, gnb_ref,
                       y_ref, st_out_ref, *, heads, dk, dv, seqs, t):
    rows = seqs * t
    q = q_ref[...]
    k = k_ref[...]
    kz = k * zeta_ref[...]
    kb = k.astype(BF16)
    lane = lax.broadcasted_iota(jnp.int32, q.shape, 1)

    def head_mask(h):
        return (lane >= h * dk) & (lane < (h + 1) * dk)

    q_all = jnp.concatenate([jnp.where(head_mask(h), q, jnp.zeros_like(q)) for h in range(heads)], axis=0)
    k_all = jnp.concatenate([jnp.where(head_mask(h), kb, jnp.zeros_like(kb)) for h in range(heads)], axis=0)
    kz_all = jnp.concatenate([jnp.where(head_mask(h), kz, jnp.zeros_like(kz)) for h in range(heads)], axis=0)
    v_all = jnp.concatenate([v_ref[:, h * dv:(h + 1) * dv] for h in range(heads)], axis=0)

    inner = _dot_nt(q_all, k_all) * dm_ref[...]
    o = _dot(inner.astype(BF16), v_all)
    kzt = kz_all.T
    n = heads * rows
    rid = lax.broadcasted_iota(jnp.int32, (n, dv), 0)
    cid = lax.broadcasted_iota(jnp.int32, kzt.shape, 1)

    def of_seq(idx, s):
        hit = (idx >= s * t) & (idx < (s + 1) * t)
        for h in range(1, heads):
            hit = hit | ((idx >= h * rows + s * t) & (idx < h * rows + (s + 1) * t))
        return hit

    o_state = jnp.zeros((n, dv), F32)
    for s in range(seqs):
        s0 = st_ref[s]
        o_state = o_state + jnp.where(of_seq(rid, s), _dot(q_all, s0.astype(BF16)), 0.0)
        ds = _dot(jnp.where(of_seq(cid, s), kzt, 0.0).astype(BF16), v_all)
        st_out_ref[s] = gc_ref[...] * s0 + ds
    o = o + o_state * xi_ref[...]

    mu = jnp.mean(o, axis=-1, keepdims=True)
    oc = o - mu
    var = jnp.mean(oc * oc, axis=-1, keepdims=True)
    yn = oc * lax.rsqrt(var + GN_EPS)
    for h in range(heads):
        cs = slice(h * dv, (h + 1) * dv)
        y = yn[h * rows:(h + 1) * rows, :] * gng_ref[:, cs] + gnb_ref[:, cs]
        y_ref[:, cs] = (_silu(g_ref[:, cs]) * y).astype(y_ref.dtype)


def _ret_sample(qr, kr, vr, g, state, dm, xi, zeta, gc, gn_g, gn_b, *, heads, dk, dv, t, seqs):
    n_rows = qr.shape[0]
    n_seq = n_rows // t
    rows = seqs * t
    grid = (n_seq // seqs,)

    def row_spec(w):
        return pl.BlockSpec((rows, w), lambda i: (i, 0))

    def const_spec(shape):
        return pl.BlockSpec(shape, lambda i: (0,) * len(shape))

    st_spec = pl.BlockSpec((seqs, heads * dk, dv), lambda i: (i, 0, 0))
    return pl.pallas_call(
        functools.partial(_ret_sample_kernel, heads=heads, dk=dk, dv=dv, seqs=seqs, t=t),
        grid=grid,
        in_specs=[row_spec(heads * dk), row_spec(heads * dk), row_spec(heads * dv), row_spec(heads * dv),
                  st_spec, const_spec(dm.shape), const_spec(xi.shape), const_spec(zeta.shape),
                  const_spec(gc.shape), const_spec(gn_g.shape), const_spec(gn_b.shape)],
        out_specs=[row_spec(heads * dv), st_spec],
        out_shape=[jax.ShapeDtypeStruct((n_rows, heads * dv), BF16),
                   jax.ShapeDtypeStruct(state.shape, F32)],
        compiler_params=_params(("parallel",)),
        name="retention_sample",
    )(qr, kr, vr, g, state, dm, xi, zeta, gc, gn_g, gn_b)


def _fox_sample_kernel(pt_ref, q_ref, kn_ref, vn_ref, lfn_ref, kt_hbm, vt_hbm, lf_hbm,
                       x_ref, of_ref, yr_ref, wof_ref, wor_ref, g1_ref, b1_ref, wg_ref, wu_ref, wd_ref, g2_ref, b2_ref,
                       o_ref, y_ref,
                       kbuf, vbuf, lfbuf, sem, qbd_ref, qbdb_ref, m_ref, l_ref, acc_ref, carry_ref,
                       h_ref, hb_ref, f_ref,
                       *, pps, heads, dh, t, page, alpha, ff_chunks, steps_per_tile):
    b = pl.program_id(0)
    j = pl.program_id(1)
    n_steps = pl.num_programs(1)
    step = b * n_steps + j
    slot = lax.rem(step, 2)
    w = heads * dh

    def page_copies(seq, group, buf_slot):
        copies = []
        for i in range(pps):
            pid = pt_ref[seq, group * pps + i]
            copies.append(pltpu.make_async_copy(kt_hbm.at[pid], kbuf.at[buf_slot, i], sem.at[buf_slot, 0]))
            copies.append(pltpu.make_async_copy(vt_hbm.at[pid], vbuf.at[buf_slot, i], sem.at[buf_slot, 1]))
            copies.append(pltpu.make_async_copy(lf_hbm.at[pid], lfbuf.at[buf_slot, i], sem.at[buf_slot, 2]))
        return copies

    @pl.when(step == 0)
    def _():
        for c in page_copies(0, 0, 0):
            c.start()

    @pl.when(step + 1 < pl.num_programs(0) * n_steps)
    def _():
        wrap = j == n_steps - 1
        for c in page_copies(jnp.where(wrap, b + 1, b), jnp.where(wrap, 0, j + 1), 1 - slot):
            c.start()

    phase = lax.rem(step, steps_per_tile)

    @pl.when(phase == 0)
    def _():
        mix = _dot(of_ref[...], wof_ref[...]) + _dot(yr_ref[...], wor_ref[...])
        h = _layer_norm(alpha * x_ref[...] + mix, g1_ref[...], b1_ref[...])
        h_ref[...] = h
        hb_ref[...] = h.astype(BF16)

    for c, (lo, width) in enumerate(ff_chunks):
        @pl.when(phase == c + 1)
        def _(c=c, lo=lo, width=width):
            hb = hb_ref[...]
            act = (_silu(_dot(hb, wg_ref[:, lo:lo + width])) * _dot(hb, wu_ref[:, lo:lo + width])).astype(BF16)
            part = _dot(act, wd_ref[lo:lo + width, :])
            if c == 0:
                f_ref[...] = part
            else:
                f_ref[...] += part

    @pl.when(phase == len(ff_chunks) + 1)
    def _():
        y_ref[...] = _layer_norm(alpha * h_ref[...] + f_ref[...], g2_ref[...], b2_ref[...])

    for c in page_copies(b, j, slot):
        c.wait()
    kt_refs = [kbuf.at[slot, i] for i in range(pps)]
    vt_refs = [vbuf.at[slot, i] for i in range(pps)]
    lf_refs = [lfbuf.at[slot, i] for i in range(pps)]
    head_rows = lax.broadcasted_iota(jnp.int32, (heads, w), 0)
    head_lanes = lax.broadcasted_iota(jnp.int32, (heads, w), 1)
    own = (head_lanes >= head_rows * dh) & (head_lanes < (head_rows + 1) * dh)

    def tile_tokens(a):
        return jnp.concatenate([a] * t, axis=0)

    @pl.when(j == 0)
    def _():
        q = q_ref[...]
        qbd = jnp.concatenate(
            [jnp.where(own, jnp.broadcast_to(q[i:i + 1, :], (heads, w)), 0.0) for i in range(t)], axis=0)
        qbd_ref[...] = qbd
        qbdb_ref[...] = qbd.astype(BF16)
        m_ref[...] = jnp.full_like(m_ref, NEG_INF)
        l_ref[...] = jnp.zeros_like(l_ref)
        acc_ref[...] = jnp.zeros_like(acc_ref)
        carry_ref[...] = jnp.zeros_like(carry_ref)

    qbdb = qbdb_ref[...]
    carry = carry_ref[...]
    s_parts = []
    for i in range(pps):
        wc = _lane_cumsum(lf_refs[i][...])
        s_parts.append(_dot(qbdb, kt_refs[i][...].astype(BF16)) - tile_tokens(carry + wc))
        carry = carry + jnp.broadcast_to(wc[:, page - 1:page], wc.shape)
    carry_ref[...] = carry
    s = jnp.concatenate(s_parts, axis=1)
    m_old = m_ref[...]
    m_new = jnp.maximum(m_old, jnp.max(s, axis=1, keepdims=True))
    alpha = jnp.exp(m_old - m_new)
    p = jnp.exp(s - m_new)
    vt = jnp.concatenate([vt_refs[i][...].astype(BF16) for i in range(pps)], axis=1)
    l_ref[...] = alpha * l_ref[...] + jnp.sum(p, axis=1, keepdims=True)
    acc_ref[...] = alpha * acc_ref[...] + _dot_nt(p.astype(BF16), vt)
    m_ref[...] = m_new

    @pl.when(j == pl.num_programs(1) - 1)
    def _():
        qbd = qbd_ref[...]
        kn = kn_ref[...]
        vn = vn_ref[...]
        bias = carry_ref[...] + _lane_cumsum(lfn_ref[...])
        row = lax.broadcasted_iota(jnp.int32, (t * heads, 1), 0)
        cols = []
        for u in range(t):
            s_u = jnp.sum(qbd * kn[u:u + 1, :], axis=1, keepdims=True) - tile_tokens(bias[:, u:u + 1])
            cols.append(jnp.where(row >= u * heads, s_u, NEG_INF))
        m_old = m_ref[...]
        m_new = m_old
        for s_u in cols:
            m_new = jnp.maximum(m_new, s_u)
        alpha = jnp.exp(m_old - m_new)
        l_new = alpha * l_ref[...]
        acc = alpha * acc_ref[...]
        for u, s_u in enumerate(cols):
            p_u = jnp.exp(s_u - m_new)
            l_new = l_new + p_u
            acc = acc + p_u * vn[u:u + 1, :]
        o = acc / l_new
        o = jnp.where(own[None], o.reshape(t, heads, w), 0.0)
        o_ref[...] = jnp.sum(o, axis=1).astype(o_ref.dtype)


def _ff_chunks(d_ff, n):
    units = d_ff // LANES
    per = -(-units // n)
    chunks = []
    lo = 0
    while lo < units:
        width = min(per, units - lo)
        chunks.append((lo * LANES, width * LANES))
        lo += width
    return tuple(chunks)


def _fox_sample_with_finish(page_table, q, k_new, v_new, lfn_pad, cache_kt, cache_vt, cache_lft,
                            x, o_fox, yr, wof, wor, ln1_g, ln1_b, w_gate, w_up, w_down, ln2_g, ln2_b,
                            *, pps, alpha, tm):
    n_seq, t, w = q.shape
    n_pages = page_table.shape[1]
    _, heads, page = cache_lft.shape
    dh = w // heads
    n_rows, d = x.shape
    d_ff = w_gate.shape[1]
    assert page == LANES and n_pages % pps == 0 and d_ff % LANES == 0
    grid = (n_seq, n_pages // pps)
    n_tiles = n_rows // tm
    total_steps = grid[0] * grid[1]
    assert n_rows % tm == 0 and total_steps % n_tiles == 0
    steps_per_tile = total_steps // n_tiles
    assert steps_per_tile >= 3
    ff_chunks = _ff_chunks(d_ff, steps_per_tile - 2)

    def seq_spec(shape):
        return pl.BlockSpec((None,) + shape, lambda b, j, pt: (b, 0, 0))

    def tile_spec(width):
        return pl.BlockSpec((tm, width), lambda b, j, pt: ((b * grid[1] + j) // steps_per_tile, 0))

    def const_spec(a):
        return pl.BlockSpec(a.shape, lambda b, j, pt: (0,) * a.ndim, pipeline_mode=pl.Buffered(1))

    hbm = pl.BlockSpec(memory_space=pl.ANY)
    consts = (wof, wor, ln1_g, ln1_b, w_gate, w_up, w_down, ln2_g, ln2_b)
    in_specs = ([seq_spec((t, w)), seq_spec((t, w)), seq_spec((t, w)), seq_spec((heads, page)), hbm, hbm, hbm,
                 tile_spec(d), tile_spec(o_fox.shape[1]), tile_spec(yr.shape[1])] + [const_spec(a) for a in consts])
    grid_spec = pltpu.PrefetchScalarGridSpec(
        num_scalar_prefetch=1, grid=grid, in_specs=in_specs,
        out_specs=[pl.BlockSpec((None, t, w), lambda b, j, pt: (b, 0, 0)), tile_spec(d)],
        scratch_shapes=[pltpu.VMEM((2, pps, w, page), F32), pltpu.VMEM((2, pps, w, page), F32),
                        pltpu.VMEM((2, pps, heads, page), F32), pltpu.SemaphoreType.DMA((2, 3)),
                        pltpu.VMEM((t * heads, w), F32), pltpu.VMEM((t * heads, w), BF16),
                        pltpu.VMEM((t * heads, 1), F32), pltpu.VMEM((t * heads, 1), F32),
                        pltpu.VMEM((t * heads, w), F32), pltpu.VMEM((heads, page), F32),
                        pltpu.VMEM((tm, d), F32), pltpu.VMEM((tm, d), BF16), pltpu.VMEM((tm, d), F32)])
    return pl.pallas_call(
        functools.partial(_fox_sample_kernel, pps=pps, heads=heads, dh=dh, t=t, page=page, alpha=alpha,
                          ff_chunks=ff_chunks, steps_per_tile=steps_per_tile),
        grid_spec=grid_spec,
        out_shape=[jax.ShapeDtypeStruct((n_seq, t, w), BF16), jax.ShapeDtypeStruct((n_rows, d), F32)],
        compiler_params=_params(("arbitrary", "arbitrary")),
        name="fox_sample_attention_and_prompt_finish",
    )(page_table, q, k_new, v_new, lfn_pad, cache_kt, cache_vt, cache_lft, x, o_fox, yr, *consts)


def _finish_kernel(x_ref, of_ref, yr_ref, wof_ref, wor_ref, g1_ref, b1_ref, wg_ref, wu_ref, wd_ref,
                   g2_ref, b2_ref, y_ref, *, alpha):
    mix = _dot(of_ref[...], wof_ref[...]) + _dot(yr_ref[...], wor_ref[...])
    h = _layer_norm(alpha * x_ref[...] + mix, g1_ref[...], b1_ref[...])
    hb = h.astype(BF16)
    act = (_silu(_dot(hb, wg_ref[...])) * _dot(hb, wu_ref[...])).astype(BF16)
    f = _dot(act, wd_ref[...])
    y_ref[...] = _layer_norm(alpha * h + f, g2_ref[...], b2_ref[...])


def _finish(x, o_fox, yr, wof, wor, ln1_g, ln1_b, w_gate, w_up, w_down, ln2_g, ln2_b, *, alpha, tm):
    n, d = x.shape
    tm = min(tm, n)

    def row_spec(w):
        return pl.BlockSpec((tm, w), lambda i: (i, 0))

    def const_spec(a):
        return pl.BlockSpec(a.shape, lambda i: (0,) * a.ndim, pipeline_mode=pl.Buffered(1))

    consts = (wof, wor, ln1_g, ln1_b, w_gate, w_up, w_down, ln2_g, ln2_b)
    return pl.pallas_call(
        functools.partial(_finish_kernel, alpha=alpha),
        grid=(n // tm,),
        in_specs=[row_spec(d), row_spec(o_fox.shape[1]), row_spec(yr.shape[1])] + [const_spec(a) for a in consts],
        out_specs=row_spec(d),
        out_shape=jax.ShapeDtypeStruct((n, d), F32),
        compiler_params=_params(("parallel",)),
        name="finish",
    )(x, o_fox, yr, *consts)


def _rope_tables(pos, dk):
    half = dk // 2
    inv = ROPE_BASE ** (-jnp.arange(half, dtype=F32) / half)
    ang = pos.astype(F32)[:, None] * inv[None, :]
    cos = jnp.cos(ang)
    sin = jnp.sin(ang)
    reps = LANES // dk
    cos_row = jnp.tile(jnp.concatenate([cos, cos], axis=1), (1, reps))
    sin_row = jnp.tile(jnp.concatenate([-sin, sin], axis=1), (1, reps))
    return cos_row, sin_row


def _decay_tables(heads, chunk):
    f = np.float32
    log_g = np.log1p(-(f(2.0) ** (f(-5.0) - np.arange(heads, dtype=f)))).astype(f)
    idx = np.arange(chunk, dtype=f)
    diff = idx[:, None] - idx[None, :]
    dmat = np.where(diff[None] >= 0, np.exp(np.maximum(diff, f(0.0))[None] * log_g[:, None, None]), f(0.0)).astype(f)
    xi = np.exp((idx + f(1.0))[:, None] * log_g[None, :]).astype(f)
    zeta = np.exp((f(chunk) - f(1.0) - idx)[:, None] * log_g[None, :]).astype(f)
    gc = np.exp(f(chunk) * log_g).astype(f)
    return dmat, xi, zeta, gc


def kernel(x_prompt, x_sample, cache_k, cache_v, cache_logf, state_ret, page_table, w_in, b_fgate, ret_gn_g,
           ret_gn_b, w_o, ln1_g, ln1_b, w_gate, w_up, w_down, ln2_g, ln2_b):
    depth, d_model, _ = w_in.shape
    batch, seq, _ = x_prompt.shape
    n_seq, t_dec, _ = x_sample.shape
    _, n_pool, page, fox_heads, fox_dh = cache_k.shape
    _, _, ret_heads, ret_dk, ret_dv = state_ret.shape
    fox_w = fox_heads * fox_dh
    ret_qk_w = ret_heads * ret_dk
    ret_v_w = ret_heads * ret_dv
    past_len = page_table.shape[1] * page
    alpha = (2 * depth) ** 0.25
    o_ff = 3 * fox_w
    o_rq = o_ff + fox_heads
    assert LANES % ret_dk == 0 and ret_qk_w % LANES == 0 and seq % RET_CHUNK == 0 and t_dec < RET_CHUNK

    cos_p, sin_p = _rope_tables(jnp.arange(seq, dtype=jnp.int32), ret_dk)
    pos_s = past_len + jnp.arange(t_dec, dtype=jnp.int32)
    cos_s, sin_s = _rope_tables(jnp.tile(pos_s, n_seq), ret_dk)
    dmat_p, xi_p, zeta_p, gc_p = _decay_tables(ret_heads, RET_CHUNK)
    xi_p_tab = np.ascontiguousarray(np.broadcast_to(xi_p.T[:, :, None], (ret_heads, RET_CHUNK, ret_dv)))
    zeta_p_tab = np.repeat(zeta_p, ret_dk, axis=1)
    gc_p_static = tuple(float(v) for v in gc_p)

    seqs = 8
    while n_seq % seqs:
        seqs //= 2
    rows = seqs * t_dec
    dmat_s, xi_s, zeta_s, gc_s = _decay_tables(ret_heads, t_dec)
    r_h = np.repeat(np.arange(ret_heads), rows)
    r_s = np.tile(np.repeat(np.arange(seqs), t_dec), ret_heads)
    r_t = np.tile(np.arange(t_dec), ret_heads * seqs)
    dm_s_tab = np.where((r_h[:, None] == r_h[None, :]) & (r_s[:, None] == r_s[None, :]),
                        dmat_s[r_h[:, None], r_t[:, None], r_t[None, :]], np.float32(0.0)).astype(np.float32)
    xi_s_tab = np.ascontiguousarray(np.broadcast_to(xi_s[r_t, r_h][:, None], (ret_heads * rows, ret_dv)))
    zeta_s_tab = np.tile(np.repeat(zeta_s, ret_dk, axis=1), (seqs, 1))
    gc_s_tab = np.ascontiguousarray(np.broadcast_to(np.repeat(gc_s, ret_dk)[:, None], (ret_qk_w, ret_dv)))

    pps = 16
    while page_table.shape[1] % pps:
        pps //= 2

    hp = x_prompt
    hs = x_sample.reshape(1, n_seq * t_dec, d_model)
    kp_l, vp_l, lfp_l, sp_l, ks_l, vs_l, lfs_l, ss_l = [], [], [], [], [], [], [], []
    for i in range(depth):
        w = w_in[i]
        wfox = w[:, :o_ff].astype(BF16)
        wfg = jnp.pad(w[:, o_ff:o_rq], ((0, 0), (0, LANES - fox_heads))).astype(BF16)
        wret = w[:, o_rq:].astype(BF16)
        bfg = jnp.pad(b_fgate[i], (0, LANES - fox_heads))[None, :]
        wof = w_o[i][:fox_w].astype(BF16)
        wor = w_o[i][fox_w:].astype(BF16)
        gn_g = ret_gn_g[i][None, :]
        gn_b = ret_gn_b[i][None, :]
        tail = (wof, wor, ln1_g[i][None, :], ln1_b[i][None, :], w_gate[i].astype(BF16), w_up[i].astype(BF16),
                w_down[i].astype(BF16), ln2_g[i][None, :], ln2_b[i][None, :])
        proj = functools.partial(_projection, fox_heads=fox_heads, fox_dh=fox_dh, ret_heads=ret_heads,
                                 ret_dk=ret_dk, tm=512)

        qtb, kt, vt, kx, vtb, lft, qr, kr, vr, g = proj(hp, wfox, wfg, wret, bfg, cos_p, sin_p, prompt=True)
        o_f = _fox_prompt(qtb, kx, vtb, heads=fox_heads, dh=fox_dh, tq=512)
        yr, s_p = _ret_prompt(qr, kr, vr, g, dmat_p, xi_p_tab, zeta_p_tab, gn_g, gn_b, heads=ret_heads,
                              dk=ret_dk, dv=ret_dv, gc=gc_p_static, tr=512)
        kp_l.append(jnp.transpose(kt.reshape(batch, fox_heads, fox_dh, seq), (0, 3, 1, 2)))
        vp_l.append(jnp.transpose(vt.reshape(batch, fox_heads, fox_dh, seq), (0, 3, 1, 2)))
        lfp_l.append(jnp.transpose(lft, (0, 2, 1)))
        sp_l.append(s_p.reshape(batch, ret_heads, ret_dk, ret_dv).astype(state_ret.dtype))
        o_f_p = o_f.reshape(batch * seq, fox_w)
        yr_p = yr.reshape(batch * seq, ret_v_w)

        qf, kf, vf, lf, qr, kr, vr, g = proj(hs, wfox, wfg, wret, bfg, cos_s, sin_s, prompt=False)
        cache_kt = jnp.transpose(cache_k[i], (0, 2, 3, 1)).reshape(n_pool, fox_w, page)
        cache_vt = jnp.transpose(cache_v[i], (0, 2, 3, 1)).reshape(n_pool, fox_w, page)
        cache_lft = jnp.transpose(cache_logf[i], (0, 2, 1))
        lfn = jnp.swapaxes(lf.reshape(n_seq, t_dec, fox_heads), 1, 2)
        lfn_pad = jnp.pad(lfn, ((0, 0), (0, 0), (0, page - t_dec)))
        o_f, new_hp = _fox_sample_with_finish(
            page_table, qf.reshape(n_seq, t_dec, fox_w), kf.reshape(n_seq, t_dec, fox_w),
            vf.reshape(n_seq, t_dec, fox_w), lfn_pad, cache_kt, cache_vt, cache_lft,
            hp.reshape(batch * seq, d_model), o_f_p, yr_p, *tail, pps=pps, alpha=alpha, tm=256)
        yr, s_s = _ret_sample(qr[0], kr[0], vr[0], g[0], state_ret[i].reshape(n_seq, ret_qk_w, ret_dv),
                              dm_s_tab, xi_s_tab, zeta_s_tab, gc_s_tab, gn_g, gn_b, heads=ret_heads, dk=ret_dk,
                              dv=ret_dv, t=t_dec, seqs=seqs)
        ks_l.append(kf.reshape(n_seq, t_dec, fox_heads, fox_dh))
        vs_l.append(vf.reshape(n_seq, t_dec, fox_heads, fox_dh))
        lfs_l.append(lf.reshape(n_seq, t_dec, fox_heads))
        ss_l.append(s_s.reshape(n_seq, ret_heads, ret_dk, ret_dv).astype(state_ret.dtype))
        hs = _finish(hs[0], o_f.reshape(n_seq * t_dec, fox_w), yr, *tail, alpha=alpha,
                     tm=512).reshape(1, n_seq * t_dec, d_model)
        hp = new_hp.reshape(batch, seq, d_model)
    return (hp, hs.reshape(n_seq, t_dec, d_model), jnp.stack(kp_l), jnp.stack(vp_l), jnp.stack(lfp_l),
            jnp.stack(sp_l), jnp.stack(ks_l), jnp.stack(vs_l), jnp.stack(lfs_l), jnp.stack(ss_l))
```

```python
import functools

import jax
import jax.numpy as jnp
import numpy as np
from jax import lax
from jax.experimental import pallas as pl
from jax.experimental.pallas import tpu as pltpu

ROPE_BASE = 10000.0
LN_EPS = 1e-5
GN_EPS = 1e-5
RET_CHUNK = 128

LANES = 128
ONES_ROWS = 16
VMEM_LIMIT_BYTES = 56 * 1024 * 1024

BF16 = jnp.bfloat16
F32 = jnp.float32
NEG_INF = float("-inf")
LOG2E = 1.4426950408889634


def _params(semantics):
    return pltpu.CompilerParams(dimension_semantics=semantics, vmem_limit_bytes=VMEM_LIMIT_BYTES)


def _dot(a, b):
    return jnp.dot(a, b, preferred_element_type=F32)


def _dot_nt(a, b):
    return lax.dot_general(a, b, (((1,), (1,)), ((), ())), preferred_element_type=F32)


def _lane_cumsum(x):
    lane = lax.broadcasted_iota(jnp.int32, x.shape, 1)
    shift = 1
    while shift < LANES:
        x = x + jnp.where(lane >= shift, pltpu.roll(x, shift, axis=1), 0.0)
        shift *= 2
    return x


def _silu(x):
    return x / (1.0 + jnp.exp(-x))


def _layer_norm(x, g, b):
    mu = jnp.mean(x, axis=-1, keepdims=True)
    xc = x - mu
    var = jnp.mean(xc * xc, axis=-1, keepdims=True)
    return xc * lax.rsqrt(var + LN_EPS) * g + b


def _proj_kernel(x_ref, wfox_ref, wfg_ref, wret_ref, bfg_ref, cos_ref, sin_ref, *refs,
                 fox_w, fox_heads, ret_qk_w, q_scale, k_scale, prompt):
    if prompt:
        (qt_ref, kt_ref, vt_ref, kx_ref, vtb_ref, lft_ref,
         qr_ref, kr_ref, vr_ref, g_ref, carry_ref) = refs
    else:
        qf_ref, kf_ref, vf_ref, lf_ref, kt_ref, vt_ref, lft_ref, qr_ref, kr_ref, vr_ref, g_ref = refs
    xb = x_ref[...].astype(BF16)
    tm = xb.shape[0]

    pf = _dot(xb, wfox_ref[...])
    kf = pf[:, fox_w:2 * fox_w]
    vf = pf[:, 2 * fox_w:3 * fox_w]
    z = _dot(xb, wfg_ref[...]) + bfg_ref[...]
    lf = jnp.minimum(z, 0.0) - jnp.log1p(jnp.exp(-jnp.abs(z)))
    if prompt:
        qt_ref[...] = (pf[:, :fox_w] * (q_scale * LOG2E)).T.astype(BF16)
        kt_ref[...] = kf.T
        vt = vf.T
        vt_ref[...] = vt
        vtb_ref[...] = vt.astype(BF16)

        @pl.when(pl.program_id(1) == 0)
        def _():
            carry_ref[...] = jnp.zeros_like(carry_ref)

        lft = lf.T[:fox_heads, :]
        lft_ref[...] = lft
        off = carry_ref[...]
        blocks = []
        for v in range(tm // LANES):
            blk = _lane_cumsum(lft[:, v * LANES:(v + 1) * LANES]) + off
            blocks.append(blk)
            off = jnp.broadcast_to(blk[:, LANES - 1:LANES], blk.shape)
        carry_ref[...] = off
        ct = jnp.concatenate(blocks, axis=1) * LOG2E
        c = jnp.concatenate([ct, jnp.zeros((LANES - fox_heads, tm), F32)], axis=0).T
        hi = c.astype(BF16).astype(F32)
        r1 = c - hi
        mid = r1.astype(BF16).astype(F32)
        lo = (r1 - mid).astype(BF16).astype(F32)
        ext = hi + pltpu.roll(mid, fox_heads, axis=1) + pltpu.roll(lo, 2 * fox_heads, axis=1)
        kx_ref[:, :fox_w] = kf.astype(BF16)
        kx_ref[:, fox_w:] = ext.astype(BF16)
    else:
        qf_ref[...] = pf[:, :fox_w] * q_scale
        kf_ref[...] = kf
        vf_ref[...] = vf
        lf_ref[...] = lf[:, :fox_heads]
        kt_ref[...] = kf.T
        vt_ref[...] = vf.T
        lft_ref[...] = lf.T[:fox_heads, :]

    pr = _dot(xb, wret_ref[...])
    cos = cos_ref[...]
    sin = sin_ref[...]
    lane = lax.broadcasted_iota(jnp.int32, (tm, LANES), 1)
    first_half = (lane & 32) == 0

    def rope(xh):
        partner = jnp.where(first_half, pltpu.roll(xh, LANES - 32, axis=1), pltpu.roll(xh, 32, axis=1))
        return xh * cos + partner * sin

    for j in range(ret_qk_w // LANES):
        sl = slice(j * LANES, (j + 1) * LANES)
        qr_ref[:, sl] = rope(pr[:, sl]).astype(BF16)
        kr_ref[:, sl] = rope(pr[:, ret_qk_w + j * LANES:ret_qk_w + (j + 1) * LANES]) * k_scale
    v_w = (pr.shape[1] - 2 * ret_qk_w) // 2
    vr_ref[...] = pr[:, 2 * ret_qk_w:2 * ret_qk_w + v_w].astype(BF16)
    g_ref[...] = pr[:, 2 * ret_qk_w + v_w:]


def _projection(x, wfox, wfg, wret, bfg, cos_tab, sin_tab, *, fox_heads, fox_dh, ret_heads, ret_dk, prompt, tm):
    b, l, d = x.shape
    fox_w = fox_heads * fox_dh
    ret_qk_w = ret_heads * ret_dk
    ret_v_w = (wret.shape[1] - 2 * ret_qk_w) // 2
    tm = min(tm, l)
    grid = (b, l // tm)

    def row_spec(w):
        return pl.BlockSpec((None, tm, w), lambda i, j: (i, j, 0))

    def col_spec(w):
        return pl.BlockSpec((None, w, tm), lambda i, j: (i, 0, j))

    def const_spec(shape):
        return pl.BlockSpec(shape, lambda i, j: (0,) * len(shape))

    in_specs = [row_spec(d), const_spec(wfox.shape), const_spec(wfg.shape), const_spec(wret.shape),
                const_spec(bfg.shape),
                pl.BlockSpec((tm, LANES), lambda i, j: (j, 0)),
                pl.BlockSpec((tm, LANES), lambda i, j: (j, 0))]

    def rows(w, dt):
        return jax.ShapeDtypeStruct((b, l, w), dt)

    def cols(w, dt):
        return jax.ShapeDtypeStruct((b, w, l), dt)

    ret_shapes = [rows(ret_qk_w, BF16), rows(ret_qk_w, F32), rows(ret_v_w, BF16), rows(ret_v_w, F32)]
    ret_specs = [row_spec(ret_qk_w), row_spec(ret_qk_w), row_spec(ret_v_w), row_spec(ret_v_w)]
    if prompt:
        out_shape = [cols(fox_w, BF16), cols(fox_w, F32), cols(fox_w, F32), rows(fox_w + LANES, BF16),
                     cols(fox_w, BF16), cols(fox_heads, F32)] + ret_shapes
        out_specs = [col_spec(fox_w), col_spec(fox_w), col_spec(fox_w), row_spec(fox_w + LANES),
                     col_spec(fox_w), col_spec(fox_heads)] + ret_specs
        scratch = [pltpu.VMEM((fox_heads, LANES), F32)]
    else:
        out_shape = [rows(fox_w, F32), rows(fox_w, F32), rows(fox_w, F32), rows(fox_heads, F32),
                     cols(fox_w, F32), cols(fox_w, F32), cols(fox_heads, F32)] + ret_shapes
        out_specs = ([row_spec(fox_w)] * 3 + [row_spec(fox_heads), col_spec(fox_w), col_spec(fox_w),
                                              col_spec(fox_heads)] + ret_specs)
        scratch = []
    kern = functools.partial(_proj_kernel, fox_w=fox_w, fox_heads=fox_heads, ret_qk_w=ret_qk_w,
                             q_scale=fox_dh ** -0.5, k_scale=ret_dk ** -0.5, prompt=prompt)
    return pl.pallas_call(
        kern, grid=grid, in_specs=in_specs, out_specs=out_specs, out_shape=out_shape,
        scratch_shapes=scratch, compiler_params=_params(("parallel", "arbitrary")),
        name="projection_prompt" if prompt else "projection_sample",
    )(x, wfox, wfg, wret, bfg, cos_tab, sin_tab)


def _fox_prompt_kernel(qt_ref, k_ref, kx_ref, vt_ref, o_ref, qx_ref, s0_ref, s1_ref, m_ref, acc_ref,
                       *, heads, dh, tq, tk):
    pair = pl.program_id(1)
    qi = pl.program_id(2)
    qt = qt_ref[...]
    row = lax.broadcasted_iota(jnp.int32, qt.shape, 0)
    zero = jnp.zeros_like(qt)
    qx = []
    for h in range(2):
        top = jnp.where((row >= h * dh) & (row < (h + 1) * dh), qt, zero)
        hh = 2 * pair + h
        bias_rows = (row == hh) | (row == hh + heads) | (row == hh + 2 * heads)
        qx.append(jnp.concatenate([top, jnp.where(bias_rows, -1.0, 0.0).astype(qt.dtype)], axis=0))

    for h in range(2):
        qx_ref[h] = qx[h]
    m_ref[...] = jnp.full_like(m_ref, NEG_INF)
    acc_ref[...] = jnp.zeros_like(acc_ref)
    ones = jnp.ones((ONES_ROWS, tk), qt.dtype)

    def scores(kj, dst, q_lo=0):
        off = pl.multiple_of(kj * tk, tk)
        kx = jnp.concatenate([k_ref[pl.ds(off, tk), :], kx_ref[pl.ds(off, tk), :]], axis=1)
        for h in range(2):
            dst[h, :, q_lo:] = _dot(kx, qx_ref[h, :, q_lo:])

    def absorb(kj, src, diag, q_lo=0):
        off = pl.multiple_of(kj * tk, tk)
        vt = vt_ref[:, pl.ds(off, tk)]
        for h in range(2):
            s = src[h, :, q_lo:]
            if diag is not None:
                key = lax.broadcasted_iota(jnp.int32, s.shape, 0)
                qry = lax.broadcasted_iota(jnp.int32, s.shape, 1) + q_lo
                s = jnp.where(key + diag * tk <= qry, s, NEG_INF)
            m_old = m_ref[h, :, q_lo:]
            m_new = jnp.maximum(m_old, jnp.max(s, axis=0, keepdims=True))
            alpha = jnp.exp2(m_old - m_new)
            p = jnp.exp2(s - m_new)
            vt1 = jnp.concatenate([vt[h * dh:(h + 1) * dh, :], ones], axis=0)
            acc_ref[h, :, q_lo:] = alpha * acc_ref[h, :, q_lo:] + _dot(vt1, p.astype(BF16))
            m_ref[h, :, q_lo:] = m_new

    scores(0, s0_ref)

    def body(i, carry):
        kj = 2 * i
        scores(kj + 1, s1_ref)
        absorb(kj, s0_ref, None)
        scores(kj + 2, s0_ref)
        absorb(kj + 1, s1_ref, None)
        return carry

    lax.fori_loop(0, qi, body, 0)
    scores(2 * qi + 1, s1_ref, q_lo=tk)
    absorb(2 * qi, s0_ref, 0)
    absorb(2 * qi + 1, s1_ref, 1, q_lo=tk)
    ot = jnp.concatenate([acc_ref[h, :dh, :] / acc_ref[h, dh:dh + 1, :] for h in range(2)], axis=0)
    o_ref[...] = ot.T.astype(o_ref.dtype)


def _fox_prompt(qt, kx, vt, *, heads, dh, tq):
    b, w, l = qt.shape
    assert 2 * dh == LANES and heads % 2 == 0 and 3 * heads <= LANES
    tq = min(tq, l)
    tk = tq // 2
    grid = (b, heads // 2, l // tq)
    return pl.pallas_call(
        functools.partial(_fox_prompt_kernel, heads=heads, dh=dh, tq=tq, tk=tk),
        grid=grid,
        in_specs=[pl.BlockSpec((None, LANES, tq), lambda i, p, j: (i, p, j)),
                  pl.BlockSpec((None, l, LANES), lambda i, p, j: (i, 0, p)),
                  pl.BlockSpec((None, l, LANES), lambda i, p, j: (i, 0, w // LANES)),
                  pl.BlockSpec((None, LANES, l), lambda i, p, j: (i, p, 0))],
        out_specs=pl.BlockSpec((None, tq, LANES), lambda i, p, j: (i, j, p)),
        out_shape=jax.ShapeDtypeStruct((b, l, w), BF16),
        scratch_shapes=[pltpu.VMEM((2, 2 * LANES, tq), BF16),
                        pltpu.VMEM((2, tk, tq), F32), pltpu.VMEM((2, tk, tq), F32),
                        pltpu.VMEM((2, 1, tq), F32), pltpu.VMEM((2, dh + ONES_ROWS, tq), F32)],
        compiler_params=_params(("parallel", "parallel", "arbitrary")),
        name="fox_prompt_attention",
    )(qt, kx, kx, vt)


def _group_norm_gate(o, gate, gn_g, gn_b):
    mu = jnp.mean(o, axis=-1, keepdims=True)
    oc = o - mu
    var = jnp.mean(oc * oc, axis=-1, keepdims=True)
    y = oc * lax.rsqrt(var + GN_EPS) * gn_g + gn_b
    return _silu(gate) * y


def _ret_prompt_kernel(q_ref, k_ref, v_ref, g_ref, dmat_ref, xi_ref, zeta_ref, gng_ref, gnb_ref,
                       y_ref, s_out_ref, s_ref, *, heads, dk, dv, chunk, gc):
    @pl.when(pl.program_id(1) == 0)
    def _():
        s_ref[...] = jnp.zeros_like(s_ref)

    rows = q_ref.shape[0]
    n_chunks = rows // chunk
    lane = lax.broadcasted_iota(jnp.int32, (chunk, heads * dk), 1)

    def head_queries(c, h):
        q = q_ref[c * chunk:(c + 1) * chunk, :]
        return jnp.where((lane >= h * dk) & (lane < (h + 1) * dk), q, jnp.zeros_like(q))

    o_inner = {}
    s_inc = {}
    for c in range(n_chunks):
        r = slice(c * chunk, (c + 1) * chunk)
        k = k_ref[r, :]
        kb = k.astype(BF16)
        kzt = (k * zeta_ref[...]).T.astype(BF16)
        for h in range(heads):
            vh = v_ref[r, h * dv:(h + 1) * dv]
            inner = _dot_nt(head_queries(c, h), kb) * dmat_ref[h]
            o_inner[c, h] = _dot(inner.astype(BF16), vh)
            s_inc[c, h] = _dot(kzt[h * dk:(h + 1) * dk, :], vh)
    states = [s_ref[...]]
    for c in range(n_chunks):
        states.append(jnp.concatenate(
            [gc[h] * states[c][h * dk:(h + 1) * dk, :] + s_inc[c, h] for h in range(heads)], axis=0))
    s_ref[...] = states[n_chunks]
    s_out_ref[...] = states[n_chunks]
    for c in range(n_chunks):
        r = slice(c * chunk, (c + 1) * chunk)
        s_b = states[c].astype(BF16)
        for h in range(heads):
            o = o_inner[c, h] + _dot(head_queries(c, h), s_b) * xi_ref[h]
            cs = slice(h * dv, (h + 1) * dv)
            y = _group_norm_gate(o, g_ref[r, cs], gng_ref[:, cs], gnb_ref[:, cs])
            y_ref[r, cs] = y.astype(y_ref.dtype)


def _ret_prompt(qr, kr, vr, g, dmat, xi, zeta, gn_g, gn_b, *, heads, dk, dv, gc, tr):
    b, l, _ = qr.shape
    chunk = dmat.shape[-1]
    tr = min(tr, l)
    grid = (b, l // tr)

    def row_spec(w):
        return pl.BlockSpec((None, tr, w), lambda i, j: (i, j, 0))

    def const_spec(shape):
        return pl.BlockSpec(shape, lambda i, j: (0,) * len(shape))

    return pl.pallas_call(
        functools.partial(_ret_prompt_kernel, heads=heads, dk=dk, dv=dv, chunk=chunk, gc=gc),
        grid=grid,
        in_specs=[row_spec(heads * dk), row_spec(heads * dk), row_spec(heads * dv), row_spec(heads * dv),
                  const_spec(dmat.shape), const_spec(xi.shape), const_spec(zeta.shape),
                  const_spec(gn_g.shape), const_spec(gn_b.shape)],
        out_specs=[row_spec(heads * dv), pl.BlockSpec((None, heads * dk, dv), lambda i, j: (i, 0, 0))],
        out_shape=[jax.ShapeDtypeStruct((b, l, heads * dv), BF16),
                   jax.ShapeDtypeStruct((b, heads * dk, dv), F32)],
        scratch_shapes=[pltpu.VMEM((heads * dk, dv), F32)],
        compiler_params=_params(("parallel", "arbitrary")),
        name="retention_prompt",
    )(qr, kr, vr, g, dmat, xi, zeta, gn_g, gn_b)


def _ret_sample_kernel(q_ref, k_ref, v_ref, g_ref, st_ref, dm_ref, xi_ref, zeta_ref, gc_ref, gng_ref, gnb_ref,
                       y_ref, st_out_ref, *, heads, dk, dv, seqs, t):
    rows = seqs * t
    q = q_ref[...]
    k = k_ref[...]
    kz = k * zeta_ref[...]
    kb = k.astype(BF16)
    lane = lax.broadcasted_iota(jnp.int32, q.shape, 1)

    def head_mask(h):
        return (lane >= h * dk) & (lane < (h + 1) * dk)

    q_all = jnp.concatenate([jnp.where(head_mask(h), q, jnp.zeros_like(q)) for h in range(heads)], axis=0)
    k_all = jnp.concatenate([jnp.where(head_mask(h), kb, jnp.zeros_like(kb)) for h in range(heads)], axis=0)
    kz_all = jnp.concatenate([jnp.where(head_mask(h), kz, jnp.zeros_like(kz)) for h in range(heads)], axis=0)
    v_all = jnp.concatenate([v_ref[:, h * dv:(h + 1) * dv] for h in range(heads)], axis=0)

    inner = _dot_nt(q_all, k_all) * dm_ref[...]
    o = _dot(inner.astype(BF16), v_all)
    kzt = kz_all.T
    n = heads * rows
    rid = lax.broadcasted_iota(jnp.int32, (n, dv), 0)
    cid = lax.broadcasted_iota(jnp.int32, kzt.shape, 1)

    def of_seq(idx, s):
        hit = (idx >= s * t) & (idx < (s + 1) * t)
        for h in range(1, heads):
            hit = hit | ((idx >= h * rows + s * t) & (idx < h * rows + (s + 1) * t))
        return hit

    o_state = jnp.zeros((n, dv), F32)
    for s in range(seqs):
        s0 = st_ref[s]
        o_state = o_state + jnp.where(of_seq(rid, s), _dot(q_all, s0.astype(BF16)), 0.0)
        ds = _dot(jnp.where(of_seq(cid, s), kzt, 0.0).astype(BF16), v_all)
        st_out_ref[s] = gc_ref[...] * s0 + ds
    o = o + o_state * xi_ref[...]

    mu = jnp.mean(o, axis=-1, keepdims=True)
    oc = o - mu
    var = jnp.mean(oc * oc, axis=-1, keepdims=True)
    yn = oc * lax.rsqrt(var + GN_EPS)
    for h in range(heads):
        cs = slice(h * dv, (h + 1) * dv)
        y = yn[h * rows:(h + 1) * rows, :] * gng_ref[:, cs] + gnb_ref[:, cs]
        y_ref[:, cs] = (_silu(g_ref[:, cs]) * y).astype(y_ref.dtype)


def _ret_sample(qr, kr, vr, g, state, dm, xi, zeta, gc, gn_g, gn_b, *, heads, dk, dv, t, seqs):
    n_rows = qr.shape[0]
    n_seq = n_rows // t
    rows = seqs * t
    grid = (n_seq // seqs,)

    def row_spec(w):
        return pl.BlockSpec((rows, w), lambda i: (i, 0))

    def const_spec(shape):
        return pl.BlockSpec(shape, lambda i: (0,) * len(shape))

    st_spec = pl.BlockSpec((seqs, heads * dk, dv), lambda i: (i, 0, 0))
    return pl.pallas_call(
        functools.partial(_ret_sample_kernel, heads=heads, dk=dk, dv=dv, seqs=seqs, t=t),
        grid=grid,
        in_specs=[row_spec(heads * dk), row_spec(heads * dk), row_spec(heads * dv), row_spec(heads * dv),
                  st_spec, const_spec(dm.shape), const_spec(xi.shape), const_spec(zeta.shape),
                  const_spec(gc.shape), const_spec(gn_g.shape), const_spec(gn_b.shape)],
        out_specs=[row_spec(heads * dv), st_spec],
        out_shape=[jax.ShapeDtypeStruct((n_rows, heads * dv), BF16),
                   jax.ShapeDtypeStruct(state.shape, F32)],
        compiler_params=_params(("parallel",)),
        name="retention_sample",
    )(qr, kr, vr, g, state, dm, xi, zeta, gc, gn_g, gn_b)


def _fox_sample_kernel(pt_ref, q_ref, knt_ref, vnt_ref, lfnt_ref, kt_hbm, vt_hbm, lf_hbm,
                       x_ref, of_ref, yr_ref, wof_ref, wor_ref, g1_ref, b1_ref, wg_ref, wu_ref, wd_ref, g2_ref, b2_ref,
                       o_ref, y_ref,
                       kbuf, vbuf, lfbuf, sem, qbdb_ref, m_ref, l_ref, acc_ref, carry_ref,
                       h_ref, hb_ref, f_ref,
                       *, pps, heads, dh, t, page, alpha, ff_chunks, steps_per_tile):
    b = pl.program_id(0)
    j = pl.program_id(1)
    n_steps = pl.num_programs(1)
    step = b * n_steps + j
    slot = lax.rem(step, 2)
    w = heads * dh

    def page_copies(seq, group, buf_slot):
        copies = []
        for i in range(pps):
            pid = pt_ref[seq, group * pps + i]
            copies.append(pltpu.make_async_copy(kt_hbm.at[pid], kbuf.at[buf_slot, i], sem.at[buf_slot, 0]))
            copies.append(pltpu.make_async_copy(vt_hbm.at[pid], vbuf.at[buf_slot, i], sem.at[buf_slot, 1]))
            copies.append(pltpu.make_async_copy(lf_hbm.at[pid], lfbuf.at[buf_slot, i], sem.at[buf_slot, 2]))
        return copies

    @pl.when(step == 0)
    def _():
        for c in page_copies(0, 0, 0):
            c.start()

    @pl.when(step + 1 < pl.num_programs(0) * n_steps)
    def _():
        wrap = j == n_steps - 1
        for c in page_copies(jnp.where(wrap, b + 1, b), jnp.where(wrap, 0, j + 1), 1 - slot):
            c.start()

    phase = lax.rem(step, steps_per_tile)

    @pl.when(phase == 0)
    def _():
        mix = _dot(of_ref[...], wof_ref[...]) + _dot(yr_ref[...], wor_ref[...])
        h = _layer_norm(alpha * x_ref[...] + mix, g1_ref[...], b1_ref[...])
        h_ref[...] = h
        hb_ref[...] = h.astype(BF16)

    for c, (lo, width) in enumerate(ff_chunks):
        @pl.when(phase == c + 1)
        def _(c=c, lo=lo, width=width):
            hb = hb_ref[...]
            act = (_silu(_dot(hb, wg_ref[:, lo:lo + width])) * _dot(hb, wu_ref[:, lo:lo + width])).astype(BF16)
            part = _dot(act, wd_ref[lo:lo + width, :])
            if c == 0:
                f_ref[...] = part
            else:
                f_ref[...] += part

    @pl.when(phase == len(ff_chunks) + 1)
    def _():
        y_ref[...] = _layer_norm(alpha * h_ref[...] + f_ref[...], g2_ref[...], b2_ref[...])

    for c in page_copies(b, j, slot):
        c.wait()
    kt_refs = [kbuf.at[slot, i] for i in range(pps)]
    vt_refs = [vbuf.at[slot, i] for i in range(pps)]
    lf_refs = [lfbuf.at[slot, i] for i in range(pps)]
    head_rows = lax.broadcasted_iota(jnp.int32, (heads, w), 0)
    head_lanes = lax.broadcasted_iota(jnp.int32, (heads, w), 1)
    own = (head_lanes >= head_rows * dh) & (head_lanes < (head_rows + 1) * dh)

    def tile_tokens(a):
        return jnp.concatenate([a] * t, axis=0)

    @pl.when(j == 0)
    def _():
        q = q_ref[...]
        qbd = jnp.concatenate(
            [jnp.where(own, jnp.broadcast_to(q[i:i + 1, :], (heads, w)), 0.0) for i in range(t)], axis=0)
        qbdb_ref[...] = qbd.astype(BF16)
        m_ref[...] = jnp.full_like(m_ref, NEG_INF)
        l_ref[...] = jnp.zeros_like(l_ref)
        acc_ref[...] = jnp.zeros_like(acc_ref)
        carry_ref[...] = jnp.zeros_like(carry_ref)

    def attend(with_new_keys):
        qbdb = qbdb_ref[...]
        carry = carry_ref[...]
        s_parts = []
        for i in range(pps):
            wc = _lane_cumsum(lf_refs[i][...])
            s_parts.append(_dot(qbdb, kt_refs[i][...].astype(BF16)) - tile_tokens(carry + wc))
            carry = carry + jnp.broadcast_to(wc[:, page - 1:page], wc.shape)
        carry_ref[...] = carry
        v_parts = [vt_refs[i][...].astype(BF16) for i in range(pps)]
        if with_new_keys:
            off = lax.rem(b * t, LANES)
            pos = lax.broadcasted_iota(jnp.int32, (heads, page), 1)
            lf_new = jnp.where((pos >= off) & (pos < off + t), lfnt_ref[...], 0.0)
            s_new = _dot(qbdb, knt_ref[...].astype(BF16)) - tile_tokens(carry + _lane_cumsum(lf_new))
            row = lax.broadcasted_iota(jnp.int32, s_new.shape, 0)
            col = lax.broadcasted_iota(jnp.int32, s_new.shape, 1)
            visible = (col >= off) & ((col - off) * heads <= row)
            s_parts.append(jnp.where(visible, s_new, NEG_INF))
            v_parts.append(vnt_ref[...].astype(BF16))
        s = jnp.concatenate(s_parts, axis=1)
        m_old = m_ref[...]
        m_new = jnp.maximum(m_old, jnp.max(s, axis=1, keepdims=True))
        alpha = jnp.exp(m_old - m_new)
        p = jnp.exp(s - m_new)
        l_new = alpha * l_ref[...] + jnp.sum(p, axis=1, keepdims=True)
        acc = alpha * acc_ref[...] + _dot_nt(p.astype(BF16), jnp.concatenate(v_parts, axis=1))
        m_ref[...] = m_new
        l_ref[...] = l_new
        acc_ref[...] = acc
        return l_new, acc

    last_group = j == n_steps - 1

    @pl.when(jnp.logical_not(last_group))
    def _():
        attend(False)

    @pl.when(last_group)
    def _():
        l_new, acc = attend(True)
        o = acc / l_new
        o = jnp.where(own[None], o.reshape(t, heads, w), 0.0)
        o_ref[...] = jnp.sum(o, axis=1).astype(o_ref.dtype)


def _ff_chunks(d_ff, n):
    units = d_ff // LANES
    per = -(-units // n)
    chunks = []
    lo = 0
    while lo < units:
        width = min(per, units - lo)
        chunks.append((lo * LANES, width * LANES))
        lo += width
    return tuple(chunks)


def _fox_sample_with_finish(page_table, q, knt, vnt, lfnt, cache_kt, cache_vt, cache_lft,
                            x, o_fox, yr, wof, wor, ln1_g, ln1_b, w_gate, w_up, w_down, ln2_g, ln2_b,
                            *, pps, alpha, tm):
    n_seq, t, w = q.shape
    n_pages = page_table.shape[1]
    _, heads, page = cache_lft.shape
    dh = w // heads
    n_rows, d = x.shape
    d_ff = w_gate.shape[1]
    assert page == LANES and n_pages % pps == 0 and d_ff % LANES == 0 and (n_seq * t) % LANES == 0 and LANES % t == 0
    grid = (n_seq, n_pages // pps)
    n_tiles = n_rows // tm
    total_steps = grid[0] * grid[1]
    assert n_rows % tm == 0 and total_steps % n_tiles == 0
    steps_per_tile = total_steps // n_tiles
    assert steps_per_tile >= 3
    ff_chunks = _ff_chunks(d_ff, steps_per_tile - 2)

    def seq_spec(shape):
        return pl.BlockSpec((None,) + shape, lambda b, j, pt: (b, 0, 0))

    def tile_spec(width):
        return pl.BlockSpec((tm, width), lambda b, j, pt: ((b * grid[1] + j) // steps_per_tile, 0))

    def const_spec(a):
        return pl.BlockSpec(a.shape, lambda b, j, pt: (0,) * a.ndim, pipeline_mode=pl.Buffered(1))

    hbm = pl.BlockSpec(memory_space=pl.ANY)
    consts = (wof, wor, ln1_g, ln1_b, w_gate, w_up, w_down, ln2_g, ln2_b)
    def new_keys_spec(rows):
        return pl.BlockSpec((rows, LANES), lambda b, j, pt: (0, (b * t) // LANES))

    in_specs = ([seq_spec((t, w)), new_keys_spec(w), new_keys_spec(w), new_keys_spec(heads), hbm, hbm, hbm,
                 tile_spec(d), tile_spec(o_fox.shape[1]), tile_spec(yr.shape[1])] + [const_spec(a) for a in consts])
    grid_spec = pltpu.PrefetchScalarGridSpec(
        num_scalar_prefetch=1, grid=grid, in_specs=in_specs,
        out_specs=[pl.BlockSpec((None, t, w), lambda b, j, pt: (b, 0, 0)), tile_spec(d)],
        scratch_shapes=[pltpu.VMEM((2, pps, w, page), F32), pltpu.VMEM((2, pps, w, page), F32),
                        pltpu.VMEM((2, pps, heads, page), F32), pltpu.SemaphoreType.DMA((2, 3)),
                        pltpu.VMEM((t * heads, w), BF16),
                        pltpu.VMEM((t * heads, 1), F32), pltpu.VMEM((t * heads, 1), F32),
                        pltpu.VMEM((t * heads, w), F32), pltpu.VMEM((heads, page), F32),
                        pltpu.VMEM((tm, d), F32), pltpu.VMEM((tm, d), BF16), pltpu.VMEM((tm, d), F32)])
    return pl.pallas_call(
        functools.partial(_fox_sample_kernel, pps=pps, heads=heads, dh=dh, t=t, page=page, alpha=alpha,
                          ff_chunks=ff_chunks, steps_per_tile=steps_per_tile),
        grid_spec=grid_spec,
        out_shape=[jax.ShapeDtypeStruct((n_seq, t, w), BF16), jax.ShapeDtypeStruct((n_rows, d), F32)],
        compiler_params=_params(("arbitrary", "arbitrary")),
        name="fox_sample_attention_and_prompt_finish",
    )(page_table, q, knt, vnt, lfnt, cache_kt, cache_vt, cache_lft, x, o_fox, yr, *consts)


def _finish_kernel(x_ref, of_ref, yr_ref, wof_ref, wor_ref, g1_ref, b1_ref, wg_ref, wu_ref, wd_ref,
                   g2_ref, b2_ref, y_ref, *, alpha):
    mix = _dot(of_ref[...], wof_ref[...]) + _dot(yr_ref[...], wor_ref[...])
    h = _layer_norm(alpha * x_ref[...] + mix, g1_ref[...], b1_ref[...])
    hb = h.astype(BF16)
    act = (_silu(_dot(hb, wg_ref[...])) * _dot(hb, wu_ref[...])).astype(BF16)
    f = _dot(act, wd_ref[...])
    y_ref[...] = _layer_norm(alpha * h + f, g2_ref[...], b2_ref[...])


def _finish(x, o_fox, yr, wof, wor, ln1_g, ln1_b, w_gate, w_up, w_down, ln2_g, ln2_b, *, alpha, tm):
    n, d = x.shape
    tm = min(tm, n)

    def row_spec(w):
        return pl.BlockSpec((tm, w), lambda i: (i, 0))

    def const_spec(a):
        return pl.BlockSpec(a.shape, lambda i: (0,) * a.ndim, pipeline_mode=pl.Buffered(1))

    consts = (wof, wor, ln1_g, ln1_b, w_gate, w_up, w_down, ln2_g, ln2_b)
    return pl.pallas_call(
        functools.partial(_finish_kernel, alpha=alpha),
        grid=(n // tm,),
        in_specs=[row_spec(d), row_spec(o_fox.shape[1]), row_spec(yr.shape[1])] + [const_spec(a) for a in consts],
        out_specs=row_spec(d),
        out_shape=jax.ShapeDtypeStruct((n, d), F32),
        compiler_params=_params(("parallel",)),
        name="finish",
    )(x, o_fox, yr, *consts)


def _rope_tables(pos, dk):
    half = dk // 2
    inv = ROPE_BASE ** (-jnp.arange(half, dtype=F32) / half)
    ang = pos.astype(F32)[:, None] * inv[None, :]
    cos = jnp.cos(ang)
    sin = jnp.sin(ang)
    reps = LANES // dk
    cos_row = jnp.tile(jnp.concatenate([cos, cos], axis=1), (1, reps))
    sin_row = jnp.tile(jnp.concatenate([-sin, sin], axis=1), (1, reps))
    return cos_row, sin_row


def _decay_tables(heads, chunk):
    f = np.float32
    log_g = np.log1p(-(f(2.0) ** (f(-5.0) - np.arange(heads, dtype=f)))).astype(f)
    idx = np.arange(chunk, dtype=f)
    diff = idx[:, None] - idx[None, :]
    dmat = np.where(diff[None] >= 0, np.exp(np.maximum(diff, f(0.0))[None] * log_g[:, None, None]), f(0.0)).astype(f)
    xi = np.exp((idx + f(1.0))[:, None] * log_g[None, :]).astype(f)
    zeta = np.exp((f(chunk) - f(1.0) - idx)[:, None] * log_g[None, :]).astype(f)
    gc = np.exp(f(chunk) * log_g).astype(f)
    return dmat, xi, zeta, gc


def kernel(x_prompt, x_sample, cache_k, cache_v, cache_logf, state_ret, page_table, w_in, b_fgate, ret_gn_g,
           ret_gn_b, w_o, ln1_g, ln1_b, w_gate, w_up, w_down, ln2_g, ln2_b):
    depth, d_model, _ = w_in.shape
    batch, seq, _ = x_prompt.shape
    n_seq, t_dec, _ = x_sample.shape
    _, n_pool, page, fox_heads, fox_dh = cache_k.shape
    _, _, ret_heads, ret_dk, ret_dv = state_ret.shape
    fox_w = fox_heads * fox_dh
    ret_qk_w = ret_heads * ret_dk
    ret_v_w = ret_heads * ret_dv
    past_len = page_table.shape[1] * page
    alpha = (2 * depth) ** 0.25
    o_ff = 3 * fox_w
    o_rq = o_ff + fox_heads
    assert LANES % ret_dk == 0 and ret_qk_w % LANES == 0 and seq % RET_CHUNK == 0 and t_dec < RET_CHUNK

    cos_p, sin_p = _rope_tables(jnp.arange(seq, dtype=jnp.int32), ret_dk)
    pos_s = past_len + jnp.arange(t_dec, dtype=jnp.int32)
    cos_s, sin_s = _rope_tables(jnp.tile(pos_s, n_seq), ret_dk)
    dmat_p, xi_p, zeta_p, gc_p = _decay_tables(ret_heads, RET_CHUNK)
    xi_p_tab = np.ascontiguousarray(np.broadcast_to(xi_p.T[:, :, None], (ret_heads, RET_CHUNK, ret_dv)))
    zeta_p_tab = np.repeat(zeta_p, ret_dk, axis=1)
    gc_p_static = tuple(float(v) for v in gc_p)

    seqs = 8
    while n_seq % seqs:
        seqs //= 2
    rows = seqs * t_dec
    dmat_s, xi_s, zeta_s, gc_s = _decay_tables(ret_heads, t_dec)
    r_h = np.repeat(np.arange(ret_heads), rows)
    r_s = np.tile(np.repeat(np.arange(seqs), t_dec), ret_heads)
    r_t = np.tile(np.arange(t_dec), ret_heads * seqs)
    dm_s_tab = np.where((r_h[:, None] == r_h[None, :]) & (r_s[:, None] == r_s[None, :]),
                        dmat_s[r_h[:, None], r_t[:, None], r_t[None, :]], np.float32(0.0)).astype(np.float32)
    xi_s_tab = np.ascontiguousarray(np.broadcast_to(xi_s[r_t, r_h][:, None], (ret_heads * rows, ret_dv)))
    zeta_s_tab = np.tile(np.repeat(zeta_s, ret_dk, axis=1), (seqs, 1))
    gc_s_tab = np.ascontiguousarray(np.broadcast_to(np.repeat(gc_s, ret_dk)[:, None], (ret_qk_w, ret_dv)))

    pps = 16
    while page_table.shape[1] % pps:
        pps //= 2

    hp = x_prompt
    hs = x_sample.reshape(1, n_seq * t_dec, d_model)
    kp_l, vp_l, lfp_l, sp_l, ks_l, vs_l, lfs_l, ss_l = [], [], [], [], [], [], [], []
    for i in range(depth):
        w = w_in[i]
        wfox = w[:, :o_ff].astype(BF16)
        wfg = jnp.pad(w[:, o_ff:o_rq], ((0, 0), (0, LANES - fox_heads))).astype(BF16)
        wret = w[:, o_rq:].astype(BF16)
        bfg = jnp.pad(b_fgate[i], (0, LANES - fox_heads))[None, :]
        wof = w_o[i][:fox_w].astype(BF16)
        wor = w_o[i][fox_w:].astype(BF16)
        gn_g = ret_gn_g[i][None, :]
        gn_b = ret_gn_b[i][None, :]
        tail = (wof, wor, ln1_g[i][None, :], ln1_b[i][None, :], w_gate[i].astype(BF16), w_up[i].astype(BF16),
                w_down[i].astype(BF16), ln2_g[i][None, :], ln2_b[i][None, :])
        proj = functools.partial(_projection, fox_heads=fox_heads, fox_dh=fox_dh, ret_heads=ret_heads,
                                 ret_dk=ret_dk, tm=512)

        qtb, kt, vt, kx, vtb, lft, qr, kr, vr, g = proj(hp, wfox, wfg, wret, bfg, cos_p, sin_p, prompt=True)
        o_f = _fox_prompt(qtb, kx, vtb, heads=fox_heads, dh=fox_dh, tq=512)
        yr, s_p = _ret_prompt(qr, kr, vr, g, dmat_p, xi_p_tab, zeta_p_tab, gn_g, gn_b, heads=ret_heads,
                              dk=ret_dk, dv=ret_dv, gc=gc_p_static, tr=512)
        kp_l.append(jnp.transpose(kt.reshape(batch, fox_heads, fox_dh, seq), (0, 3, 1, 2)))
        vp_l.append(jnp.transpose(vt.reshape(batch, fox_heads, fox_dh, seq), (0, 3, 1, 2)))
        lfp_l.append(jnp.transpose(lft, (0, 2, 1)))
        sp_l.append(s_p.reshape(batch, ret_heads, ret_dk, ret_dv).astype(state_ret.dtype))
        o_f_p = o_f.reshape(batch * seq, fox_w)
        yr_p = yr.reshape(batch * seq, ret_v_w)

        qf, kf, vf, lf, knt, vnt, lfnt, qr, kr, vr, g = proj(hs, wfox, wfg, wret, bfg, cos_s, sin_s, prompt=False)
        cache_kt = jnp.transpose(cache_k[i], (0, 2, 3, 1)).reshape(n_pool, fox_w, page)
        cache_vt = jnp.transpose(cache_v[i], (0, 2, 3, 1)).reshape(n_pool, fox_w, page)
        cache_lft = jnp.transpose(cache_logf[i], (0, 2, 1))
        o_f, new_hp = _fox_sample_with_finish(
            page_table, qf.reshape(n_seq, t_dec, fox_w), knt[0], vnt[0], lfnt[0], cache_kt, cache_vt, cache_lft,
            hp.reshape(batch * seq, d_model), o_f_p, yr_p, *tail, pps=pps, alpha=alpha, tm=256)
        yr, s_s = _ret_sample(qr[0], kr[0], vr[0], g[0], state_ret[i].reshape(n_seq, ret_qk_w, ret_dv),
                              dm_s_tab, xi_s_tab, zeta_s_tab, gc_s_tab, gn_g, gn_b, heads=ret_heads, dk=ret_dk,
                              dv=ret_dv, t=t_dec, seqs=seqs)
        ks_l.append(kf.reshape(n_seq, t_dec, fox_heads, fox_dh))
        vs_l.append(vf.reshape(n_seq, t_dec, fox_heads, fox_dh))
        lfs_l.append(lf.reshape(n_seq, t_dec, fox_heads))
        ss_l.append(s_s.reshape(n_seq, ret_heads, ret_dk, ret_dv).astype(state_ret.dtype))
        hs = _finish(hs[0], o_f.reshape(n_seq * t_dec, fox_w), yr, *tail, alpha=alpha,
                     tm=512).reshape(1, n_seq * t_dec, d_model)
        hp = new_hp.reshape(batch, seq, d_model)
    return (hp, hs.reshape(n_seq, t_dec, d_model), jnp.stack(kp_l), jnp.stack(vp_l), jnp.stack(lfp_l),
            jnp.stack(sp_l), jnp.stack(ks_l), jnp.stack(vs_l), jnp.stack(lfs_l), jnp.stack(ss_l))
```

```python
import functools

import jax
import jax.numpy as jnp
import numpy as np
from jax import lax
from jax.experimental import pallas as pl
from jax.experimental.pallas import tpu as pltpu

ROPE_BASE = 10000.0
LN_EPS = 1e-5
GN_EPS = 1e-5
RET_CHUNK = 128

LANES = 128
ONES_ROWS = 16
VMEM_LIMIT_BYTES = 56 * 1024 * 1024

BF16 = jnp.bfloat16
F32 = jnp.float32
NEG_INF = float("-inf")
LOG2E = 1.4426950408889634


def _params(semantics):
    return pltpu.CompilerParams(dimension_semantics=semantics, vmem_limit_bytes=VMEM_LIMIT_BYTES)


def _dot(a, b):
    return jnp.dot(a, b, preferred_element_type=F32)


def _dot_nt(a, b):
    return lax.dot_general(a, b, (((1,), (1,)), ((), ())), preferred_element_type=F32)


def _lane_cumsum(x):
    lane = lax.broadcasted_iota(jnp.int32, x.shape, 1)
    shift = 1
    while shift < LANES:
        x = x + jnp.where(lane >= shift, pltpu.roll(x, shift, axis=1), 0.0)
        shift *= 2
    return x


def _silu(x):
    return x / (1.0 + jnp.exp(-x))


def _layer_norm(x, g, b):
    mu = jnp.mean(x, axis=-1, keepdims=True)
    xc = x - mu
    var = jnp.mean(xc * xc, axis=-1, keepdims=True)
    return xc * lax.rsqrt(var + LN_EPS) * g + b


def _proj_kernel(x_ref, wfox_ref, wfg_ref, wret_ref, bfg_ref, cos_ref, sin_ref, *refs,
                 fox_w, fox_heads, ret_qk_w, q_scale, k_scale, prompt):
    if prompt:
        (qt_ref, kt_ref, vt_ref, kx_ref, vtb_ref, lft_ref,
         qr_ref, kr_ref, vr_ref, g_ref, carry_ref) = refs
    else:
        qf_ref, kf_ref, vf_ref, lf_ref, kt_ref, vt_ref, lft_ref, qr_ref, kr_ref, vr_ref, g_ref = refs
    xb = x_ref[...].astype(BF16)
    tm = xb.shape[0]

    pf = _dot(xb, wfox_ref[...])
    kf = pf[:, fox_w:2 * fox_w]
    vf = pf[:, 2 * fox_w:3 * fox_w]
    z = _dot(xb, wfg_ref[...]) + bfg_ref[...]
    lf = jnp.minimum(z, 0.0) - jnp.log1p(jnp.exp(-jnp.abs(z)))
    if prompt:
        qt_ref[...] = (pf[:, :fox_w] * (q_scale * LOG2E)).T.astype(BF16)
        kt_ref[...] = kf.T
        vt = vf.T
        vt_ref[...] = vt
        vtb_ref[...] = vt.astype(BF16)

        @pl.when(pl.program_id(1) == 0)
        def _():
            carry_ref[...] = jnp.zeros_like(carry_ref)

        lft = lf.T[:fox_heads, :]
        lft_ref[...] = lft
        off = carry_ref[...]
        blocks = []
        for v in range(tm // LANES):
            blk = _lane_cumsum(lft[:, v * LANES:(v + 1) * LANES]) + off
            blocks.append(blk)
            off = jnp.broadcast_to(blk[:, LANES - 1:LANES], blk.shape)
        carry_ref[...] = off
        ct = jnp.concatenate(blocks, axis=1) * LOG2E
        c = jnp.concatenate([ct, jnp.zeros((LANES - fox_heads, tm), F32)], axis=0).T
        hi = c.astype(BF16).astype(F32)
        r1 = c - hi
        mid = r1.astype(BF16).astype(F32)
        lo = (r1 - mid).astype(BF16).astype(F32)
        ext = hi + pltpu.roll(mid, fox_heads, axis=1) + pltpu.roll(lo, 2 * fox_heads, axis=1)
        kx_ref[:, :fox_w] = kf.astype(BF16)
        kx_ref[:, fox_w:] = ext.astype(BF16)
    else:
        qf_ref[...] = pf[:, :fox_w] * q_scale
        kf_ref[...] = kf
        vf_ref[...] = vf
        lf_ref[...] = lf[:, :fox_heads]
        kt_ref[...] = kf.T
        vt_ref[...] = vf.T
        lft_ref[...] = lf.T[:fox_heads, :]

    pr = _dot(xb, wret_ref[...])
    cos = cos_ref[...]
    sin = sin_ref[...]
    lane = lax.broadcasted_iota(jnp.int32, (tm, LANES), 1)
    first_half = (lane & 32) == 0

    def rope(xh):
        partner = jnp.where(first_half, pltpu.roll(xh, LANES - 32, axis=1), pltpu.roll(xh, 32, axis=1))
        return xh * cos + partner * sin

    for j in range(ret_qk_w // LANES):
        sl = slice(j * LANES, (j + 1) * LANES)
        qr_ref[:, sl] = rope(pr[:, sl]).astype(BF16)
        kr_ref[:, sl] = rope(pr[:, ret_qk_w + j * LANES:ret_qk_w + (j + 1) * LANES]) * k_scale
    v_w = (pr.shape[1] - 2 * ret_qk_w) // 2
    vr_ref[...] = pr[:, 2 * ret_qk_w:2 * ret_qk_w + v_w].astype(BF16)
    g_ref[...] = pr[:, 2 * ret_qk_w + v_w:]


def _projection(x, wfox, wfg, wret, bfg, cos_tab, sin_tab, *, fox_heads, fox_dh, ret_heads, ret_dk, prompt, tm):
    b, l, d = x.shape
    fox_w = fox_heads * fox_dh
    ret_qk_w = ret_heads * ret_dk
    ret_v_w = (wret.shape[1] - 2 * ret_qk_w) // 2
    tm = min(tm, l)
    grid = (b, l // tm)

    def row_spec(w):
        return pl.BlockSpec((None, tm, w), lambda i, j: (i, j, 0))

    def col_spec(w):
        return pl.BlockSpec((None, w, tm), lambda i, j: (i, 0, j))

    def const_spec(shape):
        return pl.BlockSpec(shape, lambda i, j: (0,) * len(shape))

    in_specs = [row_spec(d), const_spec(wfox.shape), const_spec(wfg.shape), const_spec(wret.shape),
                const_spec(bfg.shape),
                pl.BlockSpec((tm, LANES), lambda i, j: (j, 0)),
                pl.BlockSpec((tm, LANES), lambda i, j: (j, 0))]

    def rows(w, dt):
        return jax.ShapeDtypeStruct((b, l, w), dt)

    def cols(w, dt):
        return jax.ShapeDtypeStruct((b, w, l), dt)

    ret_shapes = [rows(ret_qk_w, BF16), rows(ret_qk_w, F32), rows(ret_v_w, BF16), rows(ret_v_w, F32)]
    ret_specs = [row_spec(ret_qk_w), row_spec(ret_qk_w), row_spec(ret_v_w), row_spec(ret_v_w)]
    if prompt:
        out_shape = [cols(fox_w, BF16), cols(fox_w, F32), cols(fox_w, F32), rows(fox_w + LANES, BF16),
                     cols(fox_w, BF16), cols(fox_heads, F32)] + ret_shapes
        out_specs = [col_spec(fox_w), col_spec(fox_w), col_spec(fox_w), row_spec(fox_w + LANES),
                     col_spec(fox_w), col_spec(fox_heads)] + ret_specs
        scratch = [pltpu.VMEM((fox_heads, LANES), F32)]
    else:
        out_shape = [rows(fox_w, F32), rows(fox_w, F32), rows(fox_w, F32), rows(fox_heads, F32),
                     cols(fox_w, F32), cols(fox_w, F32), cols(fox_heads, F32)] + ret_shapes
        out_specs = ([row_spec(fox_w)] * 3 + [row_spec(fox_heads), col_spec(fox_w), col_spec(fox_w),
                                              col_spec(fox_heads)] + ret_specs)
        scratch = []
    kern = functools.partial(_proj_kernel, fox_w=fox_w, fox_heads=fox_heads, ret_qk_w=ret_qk_w,
                             q_scale=fox_dh ** -0.5, k_scale=ret_dk ** -0.5, prompt=prompt)
    return pl.pallas_call(
        kern, grid=grid, in_specs=in_specs, out_specs=out_specs, out_shape=out_shape,
        scratch_shapes=scratch, compiler_params=_params(("parallel", "arbitrary")),
        name="projection_prompt" if prompt else "projection_sample",
    )(x, wfox, wfg, wret, bfg, cos_tab, sin_tab)


def _fox_prompt_kernel(qt_ref, k_ref, kx_ref, vt_ref, o_ref, qx_ref, s0_ref, s1_ref, m_ref, acc_ref,
                       *, heads, dh, tq, tk):
    pair = pl.program_id(1)
    qi = pl.program_id(2)
    qt = qt_ref[...]
    row = lax.broadcasted_iota(jnp.int32, qt.shape, 0)
    zero = jnp.zeros_like(qt)
    qx = []
    for h in range(2):
        top = jnp.where((row >= h * dh) & (row < (h + 1) * dh), qt, zero)
        hh = 2 * pair + h
        bias_rows = (row == hh) | (row == hh + heads) | (row == hh + 2 * heads)
        qx.append(jnp.concatenate([top, jnp.where(bias_rows, -1.0, 0.0).astype(qt.dtype)], axis=0))

    for h in range(2):
        qx_ref[h] = qx[h]
    m_ref[...] = jnp.full_like(m_ref, NEG_INF)
    acc_ref[...] = jnp.zeros_like(acc_ref)
    ones = jnp.ones((ONES_ROWS, tk), qt.dtype)

    def scores(kj, dst, q_lo=0):
        off = pl.multiple_of(kj * tk, tk)
        kx = jnp.concatenate([k_ref[pl.ds(off, tk), :], kx_ref[pl.ds(off, tk), :]], axis=1)
        for h in range(2):
            dst[h, :, q_lo:] = _dot(kx, qx_ref[h, :, q_lo:])

    def absorb(kj, src, diag, q_lo=0):
        off = pl.multiple_of(kj * tk, tk)
        vt = vt_ref[:, pl.ds(off, tk)]
        for h in range(2):
            s = src[h, :, q_lo:]
            if diag is not None:
                key = lax.broadcasted_iota(jnp.int32, s.shape, 0)
                qry = lax.broadcasted_iota(jnp.int32, s.shape, 1) + q_lo
                s = jnp.where(key + diag * tk <= qry, s, NEG_INF)
            m_old = m_ref[h, :, q_lo:]
            m_new = jnp.maximum(m_old, jnp.max(s, axis=0, keepdims=True))
            alpha = jnp.exp2(m_old - m_new)
            p = jnp.exp2(s - m_new)
            vt1 = jnp.concatenate([vt[h * dh:(h + 1) * dh, :], ones], axis=0)
            acc_ref[h, :, q_lo:] = alpha * acc_ref[h, :, q_lo:] + _dot(vt1, p.astype(BF16))
            m_ref[h, :, q_lo:] = m_new

    scores(0, s0_ref)

    def body(i, carry):
        kj = 2 * i
        scores(kj + 1, s1_ref)
        absorb(kj, s0_ref, None)
        scores(kj + 2, s0_ref)
        absorb(kj + 1, s1_ref, None)
        return carry

    lax.fori_loop(0, qi, body, 0)
    scores(2 * qi + 1, s1_ref, q_lo=tk)
    absorb(2 * qi, s0_ref, 0)
    absorb(2 * qi + 1, s1_ref, 1, q_lo=tk)
    ot = jnp.concatenate([acc_ref[h, :dh, :] / acc_ref[h, dh:dh + 1, :] for h in range(2)], axis=0)
    o_ref[...] = ot.T.astype(o_ref.dtype)


def _fox_prompt(qt, kx, vt, *, heads, dh, tq):
    b, w, l = qt.shape
    assert 2 * dh == LANES and heads % 2 == 0 and 3 * heads <= LANES
    tq = min(tq, l)
    tk = tq // 2
    grid = (b, heads // 2, l // tq)
    return pl.pallas_call(
        functools.partial(_fox_prompt_kernel, heads=heads, dh=dh, tq=tq, tk=tk),
        grid=grid,
        in_specs=[pl.BlockSpec((None, LANES, tq), lambda i, p, j: (i, p, j)),
                  pl.BlockSpec((None, l, LANES), lambda i, p, j: (i, 0, p)),
                  pl.BlockSpec((None, l, LANES), lambda i, p, j: (i, 0, w // LANES)),
                  pl.BlockSpec((None, LANES, l), lambda i, p, j: (i, p, 0))],
        out_specs=pl.BlockSpec((None, tq, LANES), lambda i, p, j: (i, j, p)),
        out_shape=jax.ShapeDtypeStruct((b, l, w), BF16),
        scratch_shapes=[pltpu.VMEM((2, 2 * LANES, tq), BF16),
                        pltpu.VMEM((2, tk, tq), F32), pltpu.VMEM((2, tk, tq), F32),
                        pltpu.VMEM((2, 1, tq), F32), pltpu.VMEM((2, dh + ONES_ROWS, tq), F32)],
        compiler_params=_params(("parallel", "parallel", "arbitrary")),
        name="fox_prompt_attention",
    )(qt, kx, kx, vt)


def _group_norm_gate(o, gate, gn_g, gn_b):
    mu = jnp.mean(o, axis=-1, keepdims=True)
    oc = o - mu
    var = jnp.mean(oc * oc, axis=-1, keepdims=True)
    y = oc * lax.rsqrt(var + GN_EPS) * gn_g + gn_b
    return _silu(gate) * y


def _ret_prompt_kernel(q_ref, k_ref, v_ref, g_ref, dmat_ref, xi_ref, zeta_ref, gng_ref, gnb_ref,
                       y_ref, s_out_ref, s_ref, *, heads, dk, dv, chunk, gc):
    @pl.when(pl.program_id(1) == 0)
    def _():
        s_ref[...] = jnp.zeros_like(s_ref)

    rows = q_ref.shape[0]
    n_chunks = rows // chunk
    lane = lax.broadcasted_iota(jnp.int32, (chunk, heads * dk), 1)

    def head_queries(c, h):
        q = q_ref[c * chunk:(c + 1) * chunk, :]
        return jnp.where((lane >= h * dk) & (lane < (h + 1) * dk), q, jnp.zeros_like(q))

    o_inner = {}
    s_inc = {}
    for c in range(n_chunks):
        r = slice(c * chunk, (c + 1) * chunk)
        k = k_ref[r, :]
        kb = k.astype(BF16)
        kzt = (k * zeta_ref[...]).T.astype(BF16)
        for h in range(heads):
            vh = v_ref[r, h * dv:(h + 1) * dv]
            inner = _dot_nt(head_queries(c, h), kb) * dmat_ref[h]
            o_inner[c, h] = _dot(inner.astype(BF16), vh)
            s_inc[c, h] = _dot(kzt[h * dk:(h + 1) * dk, :], vh)
    states = [s_ref[...]]
    for c in range(n_chunks):
        states.append(jnp.concatenate(
            [gc[h] * states[c][h * dk:(h + 1) * dk, :] + s_inc[c, h] for h in range(heads)], axis=0))
    s_ref[...] = states[n_chunks]
    s_out_ref[...] = states[n_chunks]
    for c in range(n_chunks):
        r = slice(c * chunk, (c + 1) * chunk)
        s_b = states[c].astype(BF16)
        for h in range(heads):
            o = o_inner[c, h] + _dot(head_queries(c, h), s_b) * xi_ref[h]
            cs = slice(h * dv, (h + 1) * dv)
            y = _group_norm_gate(o, g_ref[r, cs], gng_ref[:, cs], gnb_ref[:, cs])
            y_ref[r, cs] = y.astype(y_ref.dtype)


def _ret_prompt(qr, kr, vr, g, dmat, xi, zeta, gn_g, gn_b, *, heads, dk, dv, gc, tr):
    b, l, _ = qr.shape
    chunk = dmat.shape[-1]
    tr = min(tr, l)
    grid = (b, l // tr)

    def row_spec(w):
        return pl.BlockSpec((None, tr, w), lambda i, j: (i, j, 0))

    def const_spec(shape):
        return pl.BlockSpec(shape, lambda i, j: (0,) * len(shape))

    return pl.pallas_call(
        functools.partial(_ret_prompt_kernel, heads=heads, dk=dk, dv=dv, chunk=chunk, gc=gc),
        grid=grid,
        in_specs=[row_spec(heads * dk), row_spec(heads * dk), row_spec(heads * dv), row_spec(heads * dv),
                  const_spec(dmat.shape), const_spec(xi.shape), const_spec(zeta.shape),
                  const_spec(gn_g.shape), const_spec(gn_b.shape)],
        out_specs=[row_spec(heads * dv), pl.BlockSpec((None, heads * dk, dv), lambda i, j: (i, 0, 0))],
        out_shape=[jax.ShapeDtypeStruct((b, l, heads * dv), BF16),
                   jax.ShapeDtypeStruct((b, heads * dk, dv), F32)],
        scratch_shapes=[pltpu.VMEM((heads * dk, dv), F32)],
        compiler_params=_params(("parallel", "arbitrary")),
        name="retention_prompt",
    )(qr, kr, vr, g, dmat, xi, zeta, gn_g, gn_b)


def _ret_sample_kernel(q_ref, k_ref, v_ref, g_ref, st_ref, dm_ref, xi_ref, zeta_ref, gc_ref, gng_ref, gnb_ref,
                       y_ref, st_out_ref, *, heads, dk, dv, seqs, t):
    rows = seqs * t
    q = q_ref[...]
    k = k_ref[...]
    kz = k * zeta_ref[...]
    kb = k.astype(BF16)
    lane = lax.broadcasted_iota(jnp.int32, q.shape, 1)

    def head_mask(h):
        return (lane >= h * dk) & (lane < (h + 1) * dk)

    q_all = jnp.concatenate([jnp.where(head_mask(h), q, jnp.zeros_like(q)) for h in range(heads)], axis=0)
    k_all = jnp.concatenate([jnp.where(head_mask(h), kb, jnp.zeros_like(kb)) for h in range(heads)], axis=0)
    kz_all = jnp.concatenate([jnp.where(head_mask(h), kz, jnp.zeros_like(kz)) for h in range(heads)], axis=0)
    v_all = jnp.concatenate([v_ref[:, h * dv:(h + 1) * dv] for h in range(heads)], axis=0)

    inner = _dot_nt(q_all, k_all) * dm_ref[...]
    o = _dot(inner.astype(BF16), v_all)
    kzt = kz_all.T
    n = heads * rows
    rid = lax.broadcasted_iota(jnp.int32, (n, dv), 0)
    cid = lax.broadcasted_iota(jnp.int32, kzt.shape, 1)

    def of_seq(idx, s):
        hit = (idx >= s * t) & (idx < (s + 1) * t)
        for h in range(1, heads):
            hit = hit | ((idx >= h * rows + s * t) & (idx < h * rows + (s + 1) * t))
        return hit

    o_state = jnp.zeros((n, dv), F32)
    for s in range(seqs):
        s0 = st_ref[s]
        o_state = o_state + jnp.where(of_seq(rid, s), _dot(q_all, s0.astype(BF16)), 0.0)
        ds = _dot(jnp.where(of_seq(cid, s), kzt, 0.0).astype(BF16), v_all)
        st_out_ref[s] = gc_ref[...] * s0 + ds
    o = o + o_state * xi_ref[...]

    mu = jnp.mean(o, axis=-1, keepdims=True)
    oc = o - mu
    var = jnp.mean(oc * oc, axis=-1, keepdims=True)
    yn = oc * lax.rsqrt(var + GN_EPS)
    for h in range(heads):
        cs = slice(h * dv, (h + 1) * dv)
        y = yn[h * rows:(h + 1) * rows, :] * gng_ref[:, cs] + gnb_ref[:, cs]
        y_ref[:, cs] = (_silu(g_ref[:, cs]) * y).astype(y_ref.dtype)


def _ret_sample(qr, kr, vr, g, state, dm, xi, zeta, gc, gn_g, gn_b, *, heads, dk, dv, t, seqs):
    n_rows = qr.shape[0]
    n_seq = n_rows // t
    rows = seqs * t
    grid = (n_seq // seqs,)

    def row_spec(w):
        return pl.BlockSpec((rows, w), lambda i: (i, 0))

    def const_spec(shape):
        return pl.BlockSpec(shape, lambda i: (0,) * len(shape))

    st_spec = pl.BlockSpec((seqs, heads * dk, dv), lambda i: (i, 0, 0))
    return pl.pallas_call(
        functools.partial(_ret_sample_kernel, heads=heads, dk=dk, dv=dv, seqs=seqs, t=t),
        grid=grid,
        in_specs=[row_spec(heads * dk), row_spec(heads * dk), row_spec(heads * dv), row_spec(heads * dv),
                  st_spec, const_spec(dm.shape), const_spec(xi.shape), const_spec(zeta.shape),
                  const_spec(gc.shape), const_spec(gn_g.shape), const_spec(gn_b.shape)],
        out_specs=[row_spec(heads * dv), st_spec],
        out_shape=[jax.ShapeDtypeStruct((n_rows, heads * dv), BF16),
                   jax.ShapeDtypeStruct(state.shape, F32)],
        compiler_params=_params(("parallel",)),
        name="retention_sample",
    )(qr, kr, vr, g, state, dm, xi, zeta, gc, gn_g, gn_b)


def _fox_sample_kernel(pt_ref, q_ref, knt_ref, vnt_ref, lfnt_ref, kt_hbm, vt_hbm, lf_hbm,
                       x_ref, of_ref, yr_ref, wof_ref, wor_ref, g1_ref, b1_ref, wg_ref, wu_ref, wd_ref, g2_ref, b2_ref,
                       o_ref, y_ref,
                       kbuf, vbuf, lfbuf, sem, qbdb_ref, m_ref, l_ref, acc_ref, carry_ref,
                       h_ref, hb_ref, f_ref,
                       *, pps, heads, dh, t, page, alpha, ff_chunks, steps_per_tile):
    b = pl.program_id(0)
    j = pl.program_id(1)
    n_steps = pl.num_programs(1)
    step = b * n_steps + j
    slot = lax.rem(step, 2)
    w = heads * dh

    def page_copies(seq, group, buf_slot):
        copies = []
        for i in range(pps):
            pid = pt_ref[seq, group * pps + i]
            copies.append(pltpu.make_async_copy(kt_hbm.at[pid], kbuf.at[buf_slot, i], sem.at[buf_slot, 0]))
            copies.append(pltpu.make_async_copy(vt_hbm.at[pid], vbuf.at[buf_slot, i], sem.at[buf_slot, 1]))
            copies.append(pltpu.make_async_copy(lf_hbm.at[pid], lfbuf.at[buf_slot, i], sem.at[buf_slot, 2]))
        return copies

    @pl.when(step == 0)
    def _():
        for c in page_copies(0, 0, 0):
            c.start()

    @pl.when(step + 1 < pl.num_programs(0) * n_steps)
    def _():
        wrap = j == n_steps - 1
        for c in page_copies(jnp.where(wrap, b + 1, b), jnp.where(wrap, 0, j + 1), 1 - slot):
            c.start()

    phase = lax.rem(step, steps_per_tile)

    @pl.when(phase == 0)
    def _():
        mix = _dot(of_ref[...], wof_ref[...]) + _dot(yr_ref[...], wor_ref[...])
        h = _layer_norm(alpha * x_ref[...] + mix, g1_ref[...], b1_ref[...])
        h_ref[...] = h
        hb_ref[...] = h.astype(BF16)

    for c, (lo, width) in enumerate(ff_chunks):
        @pl.when(phase == c + 1)
        def _(c=c, lo=lo, width=width):
            hb = hb_ref[...]
            act = (_silu(_dot(hb, wg_ref[:, lo:lo + width])) * _dot(hb, wu_ref[:, lo:lo + width])).astype(BF16)
            part = _dot(act, wd_ref[lo:lo + width, :])
            if c == 0:
                f_ref[...] = part
            else:
                f_ref[...] += part

    @pl.when(phase == len(ff_chunks) + 1)
    def _():
        y_ref[...] = _layer_norm(alpha * h_ref[...] + f_ref[...], g2_ref[...], b2_ref[...])

    for c in page_copies(b, j, slot):
        c.wait()
    kt_refs = [kbuf.at[slot, i] for i in range(pps)]
    vt_refs = [vbuf.at[slot, i] for i in range(pps)]
    lf_refs = [lfbuf.at[slot, i] for i in range(pps)]
    head_rows = lax.broadcasted_iota(jnp.int32, (heads, w), 0)
    head_lanes = lax.broadcasted_iota(jnp.int32, (heads, w), 1)
    own = (head_lanes >= head_rows * dh) & (head_lanes < (head_rows + 1) * dh)

    def tile_tokens(a):
        return jnp.concatenate([a] * t, axis=0)

    @pl.when(j == 0)
    def _():
        q = q_ref[...]
        qbd = jnp.concatenate(
            [jnp.where(own, jnp.broadcast_to(q[i:i + 1, :], (heads, w)), 0.0) for i in range(t)], axis=0)
        qbdb_ref[...] = qbd.astype(BF16)
        m_ref[...] = jnp.full_like(m_ref, NEG_INF)
        l_ref[...] = jnp.zeros_like(l_ref)
        acc_ref[...] = jnp.zeros_like(acc_ref)
        carry_ref[...] = jnp.zeros_like(carry_ref)

    def attend(with_new_keys):
        qbdb = qbdb_ref[...]
        carry = carry_ref[...]
        s_parts = []
        for i in range(pps):
            wc = _lane_cumsum(lf_refs[i][...])
            s_parts.append(_dot(qbdb, kt_refs[i][...].astype(BF16)) - tile_tokens(carry + wc))
            carry = carry + jnp.broadcast_to(wc[:, page - 1:page], wc.shape)
        carry_ref[...] = carry
        v_parts = [vt_refs[i][...].astype(BF16) for i in range(pps)]
        if with_new_keys:
            off = lax.rem(b * t, LANES)
            pos = lax.broadcasted_iota(jnp.int32, (heads, page), 1)
            lf_new = jnp.where((pos >= off) & (pos < off + t), lfnt_ref[...], 0.0)
            s_new = _dot(qbdb, knt_ref[...].astype(BF16)) - tile_tokens(carry + _lane_cumsum(lf_new))
            row = lax.broadcasted_iota(jnp.int32, s_new.shape, 0)
            col = lax.broadcasted_iota(jnp.int32, s_new.shape, 1)
            visible = (col >= off) & ((col - off) * heads <= row)
            s_parts.append(jnp.where(visible, s_new, NEG_INF))
            v_parts.append(vnt_ref[...].astype(BF16))
        s = jnp.concatenate(s_parts, axis=1)
        m_old = m_ref[...]
        m_new = jnp.maximum(m_old, jnp.max(s, axis=1, keepdims=True))
        alpha = jnp.exp(m_old - m_new)
        p = jnp.exp(s - m_new)
        l_new = alpha * l_ref[...] + jnp.sum(p, axis=1, keepdims=True)
        acc = alpha * acc_ref[...] + _dot_nt(p.astype(BF16), jnp.concatenate(v_parts, axis=1))
        m_ref[...] = m_new
        l_ref[...] = l_new
        acc_ref[...] = acc
        return l_new, acc

    last_group = j == n_steps - 1

    @pl.when(jnp.logical_not(last_group))
    def _():
        attend(False)

    @pl.when(last_group)
    def _():
        l_new, acc = attend(True)
        o = acc / l_new
        o = jnp.where(own[None], o.reshape(t, heads, w), 0.0)
        o_ref[...] = jnp.sum(o, axis=1).astype(o_ref.dtype)


def _ff_chunks(d_ff, n):
    units = d_ff // LANES
    per = -(-units // n)
    chunks = []
    lo = 0
    while lo < units:
        width = min(per, units - lo)
        chunks.append((lo * LANES, width * LANES))
        lo += width
    return tuple(chunks)


def _fox_sample_with_finish(page_table, q, knt, vnt, lfnt, cache_kt, cache_vt, cache_lft,
                            x, o_fox, yr, wof, wor, ln1_g, ln1_b, w_gate, w_up, w_down, ln2_g, ln2_b,
                            *, pps, alpha, tm):
    n_seq, t, w = q.shape
    n_pages = page_table.shape[1]
    _, heads, page = cache_lft.shape
    dh = w // heads
    n_rows, d = x.shape
    d_ff = w_gate.shape[1]
    assert page == LANES and n_pages % pps == 0 and d_ff % LANES == 0 and (n_seq * t) % LANES == 0 and LANES % t == 0
    grid = (n_seq, n_pages // pps)
    n_tiles = n_rows // tm
    total_steps = grid[0] * grid[1]
    assert n_rows % tm == 0 and total_steps % n_tiles == 0
    steps_per_tile = total_steps // n_tiles
    assert steps_per_tile >= 3
    ff_chunks = _ff_chunks(d_ff, steps_per_tile - 2)

    def seq_spec(shape):
        return pl.BlockSpec((None,) + shape, lambda b, j, pt: (b, 0, 0))

    def tile_spec(width):
        return pl.BlockSpec((tm, width), lambda b, j, pt: ((b * grid[1] + j) // steps_per_tile, 0))

    def const_spec(a):
        return pl.BlockSpec(a.shape, lambda b, j, pt: (0,) * a.ndim, pipeline_mode=pl.Buffered(1))

    hbm = pl.BlockSpec(memory_space=pl.ANY)
    consts = (wof, wor, ln1_g, ln1_b, w_gate, w_up, w_down, ln2_g, ln2_b)
    def new_keys_spec(rows):
        return pl.BlockSpec((rows, LANES), lambda b, j, pt: (0, (b * t) // LANES))

    in_specs = ([seq_spec((t, w)), new_keys_spec(w), new_keys_spec(w), new_keys_spec(heads), hbm, hbm, hbm,
                 tile_spec(d), tile_spec(o_fox.shape[1]), tile_spec(yr.shape[1])] + [const_spec(a) for a in consts])
    grid_spec = pltpu.PrefetchScalarGridSpec(
        num_scalar_prefetch=1, grid=grid, in_specs=in_specs,
        out_specs=[pl.BlockSpec((None, t, w), lambda b, j, pt: (b, 0, 0)), tile_spec(d)],
        scratch_shapes=[pltpu.VMEM((2, pps, w, page), F32), pltpu.VMEM((2, pps, w, page), F32),
                        pltpu.VMEM((2, pps, heads, page), F32), pltpu.SemaphoreType.DMA((2, 3)),
                        pltpu.VMEM((t * heads, w), BF16),
                        pltpu.VMEM((t * heads, 1), F32), pltpu.VMEM((t * heads, 1), F32),
                        pltpu.VMEM((t * heads, w), F32), pltpu.VMEM((heads, page), F32),
                        pltpu.VMEM((tm, d), F32), pltpu.VMEM((tm, d), BF16), pltpu.VMEM((tm, d), F32)])
    return pl.pallas_call(
        functools.partial(_fox_sample_kernel, pps=pps, heads=heads, dh=dh, t=t, page=page, alpha=alpha,
                          ff_chunks=ff_chunks, steps_per_tile=steps_per_tile),
        grid_spec=grid_spec,
        out_shape=[jax.ShapeDtypeStruct((n_seq, t, w), BF16), jax.ShapeDtypeStruct((n_rows, d), F32)],
        compiler_params=_params(("arbitrary", "arbitrary")),
        name="fox_sample_attention_and_prompt_finish",
    )(page_table, q, knt, vnt, lfnt, cache_kt, cache_vt, cache_lft, x, o_fox, yr, *consts)


def _finish_kernel(x_ref, of_ref, yr_ref, wof_ref, wor_ref, g1_ref, b1_ref, wg_ref, wu_ref, wd_ref,
                   g2_ref, b2_ref, y_ref, *, alpha):
    mix = _dot(of_ref[...], wof_ref[...]) + _dot(yr_ref[...], wor_ref[...])
    h = _layer_norm(alpha * x_ref[...] + mix, g1_ref[...], b1_ref[...])
    hb = h.astype(BF16)
    act = (_silu(_dot(hb, wg_ref[...])) * _dot(hb, wu_ref[...])).astype(BF16)
    f = _dot(act, wd_ref[...])
    y_ref[...] = _layer_norm(alpha * h + f, g2_ref[...], b2_ref[...])


def _finish(x, o_fox, yr, wof, wor, ln1_g, ln1_b, w_gate, w_up, w_down, ln2_g, ln2_b, *, alpha, tm):
    n, d = x.shape
    tm = min(tm, n)

    def row_spec(w):
        return pl.BlockSpec((tm, w), lambda i: (i, 0))

    def const_spec(a):
        return pl.BlockSpec(a.shape, lambda i: (0,) * a.ndim, pipeline_mode=pl.Buffered(1))

    consts = (wof, wor, ln1_g, ln1_b, w_gate, w_up, w_down, ln2_g, ln2_b)
    return pl.pallas_call(
        functools.partial(_finish_kernel, alpha=alpha),
        grid=(n // tm,),
        in_specs=[row_spec(d), row_spec(o_fox.shape[1]), row_spec(yr.shape[1])] + [const_spec(a) for a in consts],
        out_specs=row_spec(d),
        out_shape=jax.ShapeDtypeStruct((n, d), F32),
        compiler_params=_params(("parallel",)),
        name="finish",
    )(x, o_fox, yr, *consts)


def _rope_tables(pos, dk):
    half = dk // 2
    inv = ROPE_BASE ** (-jnp.arange(half, dtype=F32) / half)
    ang = pos.astype(F32)[:, None] * inv[None, :]
    cos = jnp.cos(ang)
    sin = jnp.sin(ang)
    reps = LANES // dk
    cos_row = jnp.tile(jnp.concatenate([cos, cos], axis=1), (1, reps))
    sin_row = jnp.tile(jnp.concatenate([-sin, sin], axis=1), (1, reps))
    return cos_row, sin_row


def _decay_tables(heads, chunk):
    f = np.float32
    log_g = np.log1p(-(f(2.0) ** (f(-5.0) - np.arange(heads, dtype=f)))).astype(f)
    idx = np.arange(chunk, dtype=f)
    diff = idx[:, None] - idx[None, :]
    dmat = np.where(diff[None] >= 0, np.exp(np.maximum(diff, f(0.0))[None] * log_g[:, None, None]), f(0.0)).astype(f)
    xi = np.exp((idx + f(1.0))[:, None] * log_g[None, :]).astype(f)
    zeta = np.exp((f(chunk) - f(1.0) - idx)[:, None] * log_g[None, :]).astype(f)
    gc = np.exp(f(chunk) * log_g).astype(f)
    return dmat, xi, zeta, gc


def kernel(x_prompt, x_sample, cache_k, cache_v, cache_logf, state_ret, page_table, w_in, b_fgate, ret_gn_g,
           ret_gn_b, w_o, ln1_g, ln1_b, w_gate, w_up, w_down, ln2_g, ln2_b):
    depth, d_model, _ = w_in.shape
    batch, seq, _ = x_prompt.shape
    n_seq, t_dec, _ = x_sample.shape
    _, n_pool, page, fox_heads, fox_dh = cache_k.shape
    _, _, ret_heads, ret_dk, ret_dv = state_ret.shape
    fox_w = fox_heads * fox_dh
    ret_qk_w = ret_heads * ret_dk
    ret_v_w = ret_heads * ret_dv
    past_len = page_table.shape[1] * page
    alpha = (2 * depth) ** 0.25
    o_ff = 3 * fox_w
    o_rq = o_ff + fox_heads
    assert LANES % ret_dk == 0 and ret_qk_w % LANES == 0 and seq % RET_CHUNK == 0 and t_dec < RET_CHUNK

    cos_p, sin_p = _rope_tables(jnp.arange(seq, dtype=jnp.int32), ret_dk)
    pos_s = past_len + jnp.arange(t_dec, dtype=jnp.int32)
    cos_s, sin_s = _rope_tables(jnp.tile(pos_s, n_seq), ret_dk)
    dmat_p, xi_p, zeta_p, gc_p = _decay_tables(ret_heads, RET_CHUNK)
    xi_p_tab = np.ascontiguousarray(np.broadcast_to(xi_p.T[:, :, None], (ret_heads, RET_CHUNK, ret_dv)))
    zeta_p_tab = np.repeat(zeta_p, ret_dk, axis=1)
    gc_p_static = tuple(float(v) for v in gc_p)

    seqs = 8
    while n_seq % seqs:
        seqs //= 2
    rows = seqs * t_dec
    dmat_s, xi_s, zeta_s, gc_s = _decay_tables(ret_heads, t_dec)
    r_h = np.repeat(np.arange(ret_heads), rows)
    r_s = np.tile(np.repeat(np.arange(seqs), t_dec), ret_heads)
    r_t = np.tile(np.arange(t_dec), ret_heads * seqs)
    dm_s_tab = np.where((r_h[:, None] == r_h[None, :]) & (r_s[:, None] == r_s[None, :]),
                        dmat_s[r_h[:, None], r_t[:, None], r_t[None, :]], np.float32(0.0)).astype(np.float32)
    xi_s_tab = np.ascontiguousarray(np.broadcast_to(xi_s[r_t, r_h][:, None], (ret_heads * rows, ret_dv)))
    zeta_s_tab = np.tile(np.repeat(zeta_s, ret_dk, axis=1), (seqs, 1))
    gc_s_tab = np.ascontiguousarray(np.broadcast_to(np.repeat(gc_s, ret_dk)[:, None], (ret_qk_w, ret_dv)))

    pps = 16
    while page_table.shape[1] % pps:
        pps //= 2

    hp = x_prompt
    hs = x_sample.reshape(1, n_seq * t_dec, d_model)
    kp_l, vp_l, lfp_l, sp_l, ks_l, vs_l, lfs_l, ss_l = [], [], [], [], [], [], [], []
    for i in range(depth):
        w = w_in[i]
        wfox = w[:, :o_ff].astype(BF16)
        wfg = jnp.pad(w[:, o_ff:o_rq], ((0, 0), (0, LANES - fox_heads))).astype(BF16)
        wret = w[:, o_rq:].astype(BF16)
        bfg = jnp.pad(b_fgate[i], (0, LANES - fox_heads))[None, :]
        wof = w_o[i][:fox_w].astype(BF16)
        wor = w_o[i][fox_w:].astype(BF16)
        gn_g = ret_gn_g[i][None, :]
        gn_b = ret_gn_b[i][None, :]
        tail = (wof, wor, ln1_g[i][None, :], ln1_b[i][None, :], w_gate[i].astype(BF16), w_up[i].astype(BF16),
                w_down[i].astype(BF16), ln2_g[i][None, :], ln2_b[i][None, :])
        proj = functools.partial(_projection, fox_heads=fox_heads, fox_dh=fox_dh, ret_heads=ret_heads,
                                 ret_dk=ret_dk, tm=512)

        qtb, kt, vt, kx, vtb, lft, qr, kr, vr, g = proj(hp, wfox, wfg, wret, bfg, cos_p, sin_p, prompt=True)
        o_f = _fox_prompt(qtb, kx, vtb, heads=fox_heads, dh=fox_dh, tq=1024)
        yr, s_p = _ret_prompt(qr, kr, vr, g, dmat_p, xi_p_tab, zeta_p_tab, gn_g, gn_b, heads=ret_heads,
                              dk=ret_dk, dv=ret_dv, gc=gc_p_static, tr=512)
        kp_l.append(jnp.transpose(kt.reshape(batch, fox_heads, fox_dh, seq), (0, 3, 1, 2)))
        vp_l.append(jnp.transpose(vt.reshape(batch, fox_heads, fox_dh, seq), (0, 3, 1, 2)))
        lfp_l.append(jnp.transpose(lft, (0, 2, 1)))
        sp_l.append(s_p.reshape(batch, ret_heads, ret_dk, ret_dv).astype(state_ret.dtype))
        o_f_p = o_f.reshape(batch * seq, fox_w)
        yr_p = yr.reshape(batch * seq, ret_v_w)

        qf, kf, vf, lf, knt, vnt, lfnt, qr, kr, vr, g = proj(hs, wfox, wfg, wret, bfg, cos_s, sin_s, prompt=False)
        cache_kt = jnp.transpose(cache_k[i], (0, 2, 3, 1)).reshape(n_pool, fox_w, page)
        cache_vt = jnp.transpose(cache_v[i], (0, 2, 3, 1)).reshape(n_pool, fox_w, page)
        cache_lft = jnp.transpose(cache_logf[i], (0, 2, 1))
        o_f, new_hp = _fox_sample_with_finish(
            page_table, qf.reshape(n_seq, t_dec, fox_w), knt[0], vnt[0], lfnt[0], cache_kt, cache_vt, cache_lft,
            hp.reshape(batch * seq, d_model), o_f_p, yr_p, *tail, pps=pps, alpha=alpha, tm=512)
        yr, s_s = _ret_sample(qr[0], kr[0], vr[0], g[0], state_ret[i].reshape(n_seq, ret_qk_w, ret_dv),
                              dm_s_tab, xi_s_tab, zeta_s_tab, gc_s_tab, gn_g, gn_b, heads=ret_heads, dk=ret_dk,
                              dv=ret_dv, t=t_dec, seqs=seqs)
        ks_l.append(kf.reshape(n_seq, t_dec, fox_heads, fox_dh))
        vs_l.append(vf.reshape(n_seq, t_dec, fox_heads, fox_dh))
        lfs_l.append(lf.reshape(n_seq, t_dec, fox_heads))
        ss_l.append(s_s.reshape(n_seq, ret_heads, ret_dk, ret_dv).astype(state_ret.dtype))
        hs = _finish(hs[0], o_f.reshape(n_seq * t_dec, fox_w), yr, *tail, alpha=alpha,
                     tm=512).reshape(1, n_seq * t_dec, d_model)
        hp = new_hp.reshape(batch, seq, d_model)
    return (hp, hs.reshape(n_seq, t_dec, d_model), jnp.stack(kp_l), jnp.stack(vp_l), jnp.stack(lfp_l),
            jnp.stack(sp_l), jnp.stack(ks_l), jnp.stack(vs_l), jnp.stack(lfs_l), jnp.stack(ss_l))
```

```python
import functools

import jax
import jax.numpy as jnp
import numpy as np
from jax import lax
from jax.experimental import pallas as pl
from jax.experimental.pallas import tpu as pltpu

ROPE_BASE = 10000.0
LN_EPS = 1e-5
GN_EPS = 1e-5
RET_CHUNK = 128

LANES = 128
ONES_ROWS = 16
VMEM_LIMIT_BYTES = 56 * 1024 * 1024

BF16 = jnp.bfloat16
F32 = jnp.float32
NEG_INF = float("-inf")
LOG2E = 1.4426950408889634


def _params(semantics):
    return pltpu.CompilerParams(dimension_semantics=semantics, vmem_limit_bytes=VMEM_LIMIT_BYTES)


def _dot(a, b):
    return jnp.dot(a, b, preferred_element_type=F32)


def _dot_nt(a, b):
    return lax.dot_general(a, b, (((1,), (1,)), ((), ())), preferred_element_type=F32)


def _lane_cumsum(x):
    lane = lax.broadcasted_iota(jnp.int32, x.shape, 1)
    shift = 1
    while shift < LANES:
        x = x + jnp.where(lane >= shift, pltpu.roll(x, shift, axis=1), 0.0)
        shift *= 2
    return x


def _silu(x):
    return x / (1.0 + jnp.exp(-x))


def _layer_norm(x, g, b):
    mu = jnp.mean(x, axis=-1, keepdims=True)
    xc = x - mu
    var = jnp.mean(xc * xc, axis=-1, keepdims=True)
    return xc * lax.rsqrt(var + LN_EPS) * g + b


def _proj_kernel(x_ref, wfox_ref, wfg_ref, wret_ref, bfg_ref, cos_ref, sin_ref, *refs,
                 fox_w, fox_heads, ret_qk_w, q_scale, k_scale, prompt):
    if prompt:
        (qt_ref, kt_ref, vt_ref, kx_ref, vtb_ref, lft_ref,
         qr_ref, kr_ref, vr_ref, g_ref, carry_ref) = refs
    else:
        qf_ref, kf_ref, vf_ref, lf_ref, kt_ref, vt_ref, lft_ref, qr_ref, kr_ref, vr_ref, g_ref = refs
    xb = x_ref[...].astype(BF16)
    tm = xb.shape[0]

    pf = _dot(xb, wfox_ref[...])
    kf = pf[:, fox_w:2 * fox_w]
    vf = pf[:, 2 * fox_w:3 * fox_w]
    z = _dot(xb, wfg_ref[...]) + bfg_ref[...]
    lf = jnp.minimum(z, 0.0) - jnp.log1p(jnp.exp(-jnp.abs(z)))
    if prompt:
        qt_ref[...] = (pf[:, :fox_w] * (q_scale * LOG2E)).T.astype(BF16)
        kt_ref[...] = kf.T
        vt = vf.T
        vt_ref[...] = vt
        vtb_ref[...] = vt.astype(BF16)

        @pl.when(pl.program_id(1) == 0)
        def _():
            carry_ref[...] = jnp.zeros_like(carry_ref)

        lft = lf.T[:fox_heads, :]
        lft_ref[...] = lft
        off = carry_ref[...]
        blocks = []
        for v in range(tm // LANES):
            blk = _lane_cumsum(lft[:, v * LANES:(v + 1) * LANES]) + off
            blocks.append(blk)
            off = jnp.broadcast_to(blk[:, LANES - 1:LANES], blk.shape)
        carry_ref[...] = off
        ct = jnp.concatenate(blocks, axis=1) * LOG2E
        c = jnp.concatenate([ct, jnp.zeros((LANES - fox_heads, tm), F32)], axis=0).T
        hi = c.astype(BF16).astype(F32)
        r1 = c - hi
        mid = r1.astype(BF16).astype(F32)
        lo = (r1 - mid).astype(BF16).astype(F32)
        ext = hi + pltpu.roll(mid, fox_heads, axis=1) + pltpu.roll(lo, 2 * fox_heads, axis=1)
        kx_ref[:, :fox_w] = kf.astype(BF16)
        kx_ref[:, fox_w:] = ext.astype(BF16)
    else:
        qf_ref[...] = pf[:, :fox_w] * q_scale
        kf_ref[...] = kf
        vf_ref[...] = vf
        lf_ref[...] = lf[:, :fox_heads]
        kt_ref[...] = kf.T
        vt_ref[...] = vf.T
        lft_ref[...] = lf.T[:fox_heads, :]

    pr = _dot(xb, wret_ref[...])
    cos = cos_ref[...]
    sin = sin_ref[...]
    lane = lax.broadcasted_iota(jnp.int32, (tm, LANES), 1)
    first_half = (lane & 32) == 0

    def rope(xh):
        partner = jnp.where(first_half, pltpu.roll(xh, LANES - 32, axis=1), pltpu.roll(xh, 32, axis=1))
        return xh * cos + partner * sin

    for j in range(ret_qk_w // LANES):
        sl = slice(j * LANES, (j + 1) * LANES)
        qr_ref[:, sl] = rope(pr[:, sl]).astype(BF16)
        kr_ref[:, sl] = rope(pr[:, ret_qk_w + j * LANES:ret_qk_w + (j + 1) * LANES]) * k_scale
    v_w = (pr.shape[1] - 2 * ret_qk_w) // 2
    vr_ref[...] = pr[:, 2 * ret_qk_w:2 * ret_qk_w + v_w].astype(BF16)
    g_ref[...] = pr[:, 2 * ret_qk_w + v_w:]


def _projection(x, wfox, wfg, wret, bfg, cos_tab, sin_tab, *, fox_heads, fox_dh, ret_heads, ret_dk, prompt, tm):
    b, l, d = x.shape
    fox_w = fox_heads * fox_dh
    ret_qk_w = ret_heads * ret_dk
    ret_v_w = (wret.shape[1] - 2 * ret_qk_w) // 2
    tm = min(tm, l)
    grid = (b, l // tm)

    def row_spec(w):
        return pl.BlockSpec((None, tm, w), lambda i, j: (i, j, 0))

    def col_spec(w):
        return pl.BlockSpec((None, w, tm), lambda i, j: (i, 0, j))

    def const_spec(shape):
        return pl.BlockSpec(shape, lambda i, j: (0,) * len(shape))

    in_specs = [row_spec(d), const_spec(wfox.shape), const_spec(wfg.shape), const_spec(wret.shape),
                const_spec(bfg.shape),
                pl.BlockSpec((tm, LANES), lambda i, j: (j, 0)),
                pl.BlockSpec((tm, LANES), lambda i, j: (j, 0))]

    def rows(w, dt):
        return jax.ShapeDtypeStruct((b, l, w), dt)

    def cols(w, dt):
        return jax.ShapeDtypeStruct((b, w, l), dt)

    ret_shapes = [rows(ret_qk_w, BF16), rows(ret_qk_w, F32), rows(ret_v_w, BF16), rows(ret_v_w, F32)]
    ret_specs = [row_spec(ret_qk_w), row_spec(ret_qk_w), row_spec(ret_v_w), row_spec(ret_v_w)]
    if prompt:
        out_shape = [cols(fox_w, BF16), cols(fox_w, F32), cols(fox_w, F32), rows(fox_w + LANES, BF16),
                     cols(fox_w, BF16), cols(fox_heads, F32)] + ret_shapes
        out_specs = [col_spec(fox_w), col_spec(fox_w), col_spec(fox_w), row_spec(fox_w + LANES),
                     col_spec(fox_w), col_spec(fox_heads)] + ret_specs
        scratch = [pltpu.VMEM((fox_heads, LANES), F32)]
    else:
        out_shape = [rows(fox_w, F32), rows(fox_w, F32), rows(fox_w, F32), rows(fox_heads, F32),
                     cols(fox_w, F32), cols(fox_w, F32), cols(fox_heads, F32)] + ret_shapes
        out_specs = ([row_spec(fox_w)] * 3 + [row_spec(fox_heads), col_spec(fox_w), col_spec(fox_w),
                                              col_spec(fox_heads)] + ret_specs)
        scratch = []
    kern = functools.partial(_proj_kernel, fox_w=fox_w, fox_heads=fox_heads, ret_qk_w=ret_qk_w,
                             q_scale=fox_dh ** -0.5, k_scale=ret_dk ** -0.5, prompt=prompt)
    return pl.pallas_call(
        kern, grid=grid, in_specs=in_specs, out_specs=out_specs, out_shape=out_shape,
        scratch_shapes=scratch, compiler_params=_params(("parallel", "arbitrary")),
        name="projection_prompt" if prompt else "projection_sample",
    )(x, wfox, wfg, wret, bfg, cos_tab, sin_tab)


def _fox_prompt_kernel(qt_ref, k_ref, kx_ref, vt_ref, o_ref, qx_ref, s0_ref, s1_ref, m_ref, acc_ref,
                       *, heads, dh, tq, tk):
    pair = pl.program_id(1)
    qi = pl.program_id(2)
    qt = qt_ref[...]
    row = lax.broadcasted_iota(jnp.int32, qt.shape, 0)
    zero = jnp.zeros_like(qt)
    qx = []
    for h in range(2):
        top = jnp.where((row >= h * dh) & (row < (h + 1) * dh), qt, zero)
        hh = 2 * pair + h
        bias_rows = (row == hh) | (row == hh + heads) | (row == hh + 2 * heads)
        qx.append(jnp.concatenate([top, jnp.where(bias_rows, -1.0, 0.0).astype(qt.dtype)], axis=0))

    for h in range(2):
        qx_ref[h] = qx[h]
    m_ref[...] = jnp.full_like(m_ref, NEG_INF)
    acc_ref[...] = jnp.zeros_like(acc_ref)
    ones = jnp.ones((ONES_ROWS, tk), qt.dtype)

    def scores(kj, dst, q_lo=0):
        off = pl.multiple_of(kj * tk, tk)
        kx = jnp.concatenate([k_ref[pl.ds(off, tk), :], kx_ref[pl.ds(off, tk), :]], axis=1)
        for h in range(2):
            dst[h, :, q_lo:] = _dot(kx, qx_ref[h, :, q_lo:])

    def absorb(kj, src, diag, q_lo=0):
        off = pl.multiple_of(kj * tk, tk)
        vt = vt_ref[:, pl.ds(off, tk)]
        for h in range(2):
            s = src[h, :, q_lo:]
            if diag is not None:
                key = lax.broadcasted_iota(jnp.int32, s.shape, 0)
                qry = lax.broadcasted_iota(jnp.int32, s.shape, 1) + q_lo
                s = jnp.where(key + diag * tk <= qry, s, NEG_INF)
            m_old = m_ref[h, :, q_lo:]
            m_new = jnp.maximum(m_old, jnp.max(s, axis=0, keepdims=True))
            alpha = jnp.exp2(m_old - m_new)
            p = jnp.exp2(s - m_new)
            vt1 = jnp.concatenate([vt[h * dh:(h + 1) * dh, :], ones], axis=0)
            acc_ref[h, :, q_lo:] = alpha * acc_ref[h, :, q_lo:] + _dot(vt1, p.astype(BF16))
            m_ref[h, :, q_lo:] = m_new

    scores(0, s0_ref)

    def body(i, carry):
        kj = 2 * i
        scores(kj + 1, s1_ref)
        absorb(kj, s0_ref, None)
        scores(kj + 2, s0_ref)
        absorb(kj + 1, s1_ref, None)
        return carry

    lax.fori_loop(0, qi, body, 0)
    scores(2 * qi + 1, s1_ref, q_lo=tk)
    absorb(2 * qi, s0_ref, 0)
    absorb(2 * qi + 1, s1_ref, 1, q_lo=tk)
    ot = jnp.concatenate([acc_ref[h, :dh, :] / acc_ref[h, dh:dh + 1, :] for h in range(2)], axis=0)
    o_ref[...] = ot.T.astype(o_ref.dtype)


def _fox_prompt(qt, kx, vt, *, heads, dh, tq):
    b, w, l = qt.shape
    assert 2 * dh == LANES and heads % 2 == 0 and 3 * heads <= LANES
    tq = min(tq, l)
    tk = tq // 2
    grid = (b, heads // 2, l // tq)
    return pl.pallas_call(
        functools.partial(_fox_prompt_kernel, heads=heads, dh=dh, tq=tq, tk=tk),
        grid=grid,
        in_specs=[pl.BlockSpec((None, LANES, tq), lambda i, p, j: (i, p, j)),
                  pl.BlockSpec((None, l, LANES), lambda i, p, j: (i, 0, p)),
                  pl.BlockSpec((None, l, LANES), lambda i, p, j: (i, 0, w // LANES)),
                  pl.BlockSpec((None, LANES, l), lambda i, p, j: (i, p, 0))],
        out_specs=pl.BlockSpec((None, tq, LANES), lambda i, p, j: (i, j, p)),
        out_shape=jax.ShapeDtypeStruct((b, l, w), BF16),
        scratch_shapes=[pltpu.VMEM((2, 2 * LANES, tq), BF16),
                        pltpu.VMEM((2, tk, tq), F32), pltpu.VMEM((2, tk, tq), F32),
                        pltpu.VMEM((2, 1, tq), F32), pltpu.VMEM((2, dh + ONES_ROWS, tq), F32)],
        compiler_params=_params(("parallel", "parallel", "arbitrary")),
        name="fox_prompt_attention",
    )(qt, kx, kx, vt)


def _group_norm_gate(o, gate, gn_g, gn_b):
    mu = jnp.mean(o, axis=-1, keepdims=True)
    oc = o - mu
    var = jnp.mean(oc * oc, axis=-1, keepdims=True)
    y = oc * lax.rsqrt(var + GN_EPS) * gn_g + gn_b
    return _silu(gate) * y


def _ret_prompt_kernel(q_ref, k_ref, v_ref, g_ref, dmat_ref, xi_ref, zeta_ref, gng_ref, gnb_ref,
                       y_ref, s_out_ref, s_ref, *, heads, dk, dv, chunk, gc):
    @pl.when(pl.program_id(1) == 0)
    def _():
        s_ref[...] = jnp.zeros_like(s_ref)

    rows = q_ref.shape[0]
    n_chunks = rows // chunk
    lane = lax.broadcasted_iota(jnp.int32, (chunk, heads * dk), 1)

    def head_queries(c, h):
        q = q_ref[c * chunk:(c + 1) * chunk, :]
        return jnp.where((lane >= h * dk) & (lane < (h + 1) * dk), q, jnp.zeros_like(q))

    o_inner = {}
    s_inc = {}
    for c in range(n_chunks):
        r = slice(c * chunk, (c + 1) * chunk)
        k = k_ref[r, :]
        kb = k.astype(BF16)
        kzt = (k * zeta_ref[...]).T.astype(BF16)
        for h in range(heads):
            vh = v_ref[r, h * dv:(h + 1) * dv]
            inner = _dot_nt(head_queries(c, h), kb) * dmat_ref[h]
            o_inner[c, h] = _dot(inner.astype(BF16), vh)
            s_inc[c, h] = _dot(kzt[h * dk:(h + 1) * dk, :], vh)
    states = [s_ref[...]]
    for c in range(n_chunks):
        states.append(jnp.concatenate(
            [gc[h] * states[c][h * dk:(h + 1) * dk, :] + s_inc[c, h] for h in range(heads)], axis=0))
    s_ref[...] = states[n_chunks]
    s_out_ref[...] = states[n_chunks]
    for c in range(n_chunks):
        r = slice(c * chunk, (c + 1) * chunk)
        s_b = states[c].astype(BF16)
        for h in range(heads):
            o = o_inner[c, h] + _dot(head_queries(c, h), s_b) * xi_ref[h]
            cs = slice(h * dv, (h + 1) * dv)
            y = _group_norm_gate(o, g_ref[r, cs], gng_ref[:, cs], gnb_ref[:, cs])
            y_ref[r, cs] = y.astype(y_ref.dtype)


def _ret_prompt(qr, kr, vr, g, dmat, xi, zeta, gn_g, gn_b, *, heads, dk, dv, gc, tr):
    b, l, _ = qr.shape
    chunk = dmat.shape[-1]
    tr = min(tr, l)
    grid = (b, l // tr)

    def row_spec(w):
        return pl.BlockSpec((None, tr, w), lambda i, j: (i, j, 0))

    def const_spec(shape):
        return pl.BlockSpec(shape, lambda i, j: (0,) * len(shape))

    return pl.pallas_call(
        functools.partial(_ret_prompt_kernel, heads=heads, dk=dk, dv=dv, chunk=chunk, gc=gc),
        grid=grid,
        in_specs=[row_spec(heads * dk), row_spec(heads * dk), row_spec(heads * dv), row_spec(heads * dv),
                  const_spec(dmat.shape), const_spec(xi.shape), const_spec(zeta.shape),
                  const_spec(gn_g.shape), const_spec(gn_b.shape)],
        out_specs=[row_spec(heads * dv), pl.BlockSpec((None, heads * dk, dv), lambda i, j: (i, 0, 0))],
        out_shape=[jax.ShapeDtypeStruct((b, l, heads * dv), BF16),
                   jax.ShapeDtypeStruct((b, heads * dk, dv), F32)],
        scratch_shapes=[pltpu.VMEM((heads * dk, dv), F32)],
        compiler_params=_params(("parallel", "arbitrary")),
        name="retention_prompt",
    )(qr, kr, vr, g, dmat, xi, zeta, gn_g, gn_b)


def _ret_sample_kernel(q_ref, k_ref, v_ref, g_ref, st_ref, dm_ref, xi_ref, zeta_ref, gc_ref, gng_ref, gnb_ref,
                       y_ref, st_out_ref, *, heads, dk, dv, seqs, t):
    rows = seqs * t
    q = q_ref[...]
    k = k_ref[...]
    kz = k * zeta_ref[...]
    kb = k.astype(BF16)
    lane = lax.broadcasted_iota(jnp.int32, q.shape, 1)

    def head_mask(h):
        return (lane >= h * dk) & (lane < (h + 1) * dk)

    q_all = jnp.concatenate([jnp.where(head_mask(h), q, jnp.zeros_like(q)) for h in range(heads)], axis=0)
    k_all = jnp.concatenate([jnp.where(head_mask(h), kb, jnp.zeros_like(kb)) for h in range(heads)], axis=0)
    kz_all = jnp.concatenate([jnp.where(head_mask(h), kz, jnp.zeros_like(kz)) for h in range(heads)], axis=0)
    v_all = jnp.concatenate([v_ref[:, h * dv:(h + 1) * dv] for h in range(heads)], axis=0)

    inner = _dot_nt(q_all, k_all) * dm_ref[...]
    o = _dot(inner.astype(BF16), v_all)
    kzt = kz_all.T
    n = heads * rows
    rid = lax.broadcasted_iota(jnp.int32, (n, dv), 0)
    cid = lax.broadcasted_iota(jnp.int32, kzt.shape, 1)

    def of_seq(idx, s):
        hit = (idx >= s * t) & (idx < (s + 1) * t)
        for h in range(1, heads):
            hit = hit | ((idx >= h * rows + s * t) & (idx < h * rows + (s + 1) * t))
        return hit

    o_state = jnp.zeros((n, dv), F32)
    for s in range(seqs):
        s0 = st_ref[s]
        o_state = o_state + jnp.where(of_seq(rid, s), _dot(q_all, s0.astype(BF16)), 0.0)
        ds = _dot(jnp.where(of_seq(cid, s), kzt, 0.0).astype(BF16), v_all)
        st_out_ref[s] = gc_ref[...] * s0 + ds
    o = o + o_state * xi_ref[...]

    mu = jnp.mean(o, axis=-1, keepdims=True)
    oc = o - mu
    var = jnp.mean(oc * oc, axis=-1, keepdims=True)
    yn = oc * lax.rsqrt(var + GN_EPS)
    for h in range(heads):
        cs = slice(h * dv, (h + 1) * dv)
        y = yn[h * rows:(h + 1) * rows, :] * gng_ref[:, cs] + gnb_ref[:, cs]
        y_ref[:, cs] = (_silu(g_ref[:, cs]) * y).astype(y_ref.dtype)


def _ret_sample(qr, kr, vr, g, state, dm, xi, zeta, gc, gn_g, gn_b, *, heads, dk, dv, t, seqs):
    n_rows = qr.shape[0]
    n_seq = n_rows // t
    rows = seqs * t
    grid = (n_seq // seqs,)

    def row_spec(w):
        return pl.BlockSpec((rows, w), lambda i: (i, 0))

    def const_spec(shape):
        return pl.BlockSpec(shape, lambda i: (0,) * len(shape))

    st_spec = pl.BlockSpec((seqs, heads * dk, dv), lambda i: (i, 0, 0))
    return pl.pallas_call(
        functools.partial(_ret_sample_kernel, heads=heads, dk=dk, dv=dv, seqs=seqs, t=t),
        grid=grid,
        in_specs=[row_spec(heads * dk), row_spec(heads * dk), row_spec(heads * dv), row_spec(heads * dv),
                  st_spec, const_spec(dm.shape), const_spec(xi.shape), const_spec(zeta.shape),
                  const_spec(gc.shape), const_spec(gn_g.shape), const_spec(gn_b.shape)],
        out_specs=[row_spec(heads * dv), st_spec],
        out_shape=[jax.ShapeDtypeStruct((n_rows, heads * dv), BF16),
                   jax.ShapeDtypeStruct(state.shape, F32)],
        compiler_params=_params(("parallel",)),
        name="retention_sample",
    )(qr, kr, vr, g, state, dm, xi, zeta, gc, gn_g, gn_b)


def _fox_sample_kernel(pt_ref, q_ref, knt_ref, vnt_ref, lfnt_ref, kt_hbm, vt_hbm, lf_hbm,
                       x_ref, of_ref, yr_ref, wof_ref, wor_ref, g1_ref, b1_ref, wg_ref, wu_ref, wd_ref, g2_ref, b2_ref,
                       o_ref, y_ref,
                       kbuf, vbuf, lfbuf, sem, qbdb_ref, m_ref, l_ref, acc_ref, carry_ref,
                       h_ref, hb_ref, f_ref,
                       *, pps, heads, dh, t, page, alpha, ff_chunks, steps_per_tile):
    b = pl.program_id(0)
    j = pl.program_id(1)
    n_steps = pl.num_programs(1)
    step = b * n_steps + j
    slot = lax.rem(step, 2)
    w = heads * dh

    def page_copies(seq, group, buf_slot):
        copies = []
        for i in range(pps):
            pid = pt_ref[seq, group * pps + i]
            copies.append(pltpu.make_async_copy(kt_hbm.at[pid], kbuf.at[buf_slot, i], sem.at[buf_slot, 0]))
            copies.append(pltpu.make_async_copy(vt_hbm.at[pid], vbuf.at[buf_slot, i], sem.at[buf_slot, 1]))
            copies.append(pltpu.make_async_copy(lf_hbm.at[pid], lfbuf.at[buf_slot, i], sem.at[buf_slot, 2]))
        return copies

    @pl.when(step == 0)
    def _():
        for c in page_copies(0, 0, 0):
            c.start()

    @pl.when(step + 1 < pl.num_programs(0) * n_steps)
    def _():
        wrap = j == n_steps - 1
        for c in page_copies(jnp.where(wrap, b + 1, b), jnp.where(wrap, 0, j + 1), 1 - slot):
            c.start()

    phase = lax.rem(step, steps_per_tile)

    @pl.when(phase == 0)
    def _():
        mix = _dot(of_ref[...], wof_ref[...]) + _dot(yr_ref[...], wor_ref[...])
        h = _layer_norm(alpha * x_ref[...] + mix, g1_ref[...], b1_ref[...])
        h_ref[...] = h
        hb_ref[...] = h.astype(BF16)

    for c, (lo, width) in enumerate(ff_chunks):
        @pl.when(phase == c + 1)
        def _(c=c, lo=lo, width=width):
            hb = hb_ref[...]
            act = (_silu(_dot(hb, wg_ref[:, lo:lo + width])) * _dot(hb, wu_ref[:, lo:lo + width])).astype(BF16)
            part = _dot(act, wd_ref[lo:lo + width, :])
            if c == 0:
                f_ref[...] = part
            else:
                f_ref[...] += part

    @pl.when(phase == len(ff_chunks) + 1)
    def _():
        y_ref[...] = _layer_norm(alpha * h_ref[...] + f_ref[...], g2_ref[...], b2_ref[...])

    for c in page_copies(b, j, slot):
        c.wait()
    kt_refs = [kbuf.at[slot, i] for i in range(pps)]
    vt_refs = [vbuf.at[slot, i] for i in range(pps)]
    lf_refs = [lfbuf.at[slot, i] for i in range(pps)]
    head_rows = lax.broadcasted_iota(jnp.int32, (heads, w), 0)
    head_lanes = lax.broadcasted_iota(jnp.int32, (heads, w), 1)
    own = (head_lanes >= head_rows * dh) & (head_lanes < (head_rows + 1) * dh)

    def tile_tokens(a):
        return jnp.concatenate([a] * t, axis=0)

    @pl.when(j == 0)
    def _():
        q = q_ref[...]
        qbd = jnp.concatenate(
            [jnp.where(own, jnp.broadcast_to(q[i:i + 1, :], (heads, w)), 0.0) for i in range(t)], axis=0)
        qbdb_ref[...] = qbd.astype(BF16)
        m_ref[...] = jnp.full_like(m_ref, NEG_INF)
        l_ref[...] = jnp.zeros_like(l_ref)
        acc_ref[...] = jnp.zeros_like(acc_ref)
        carry_ref[...] = jnp.zeros_like(carry_ref)

    def attend(with_new_keys):
        qbdb = qbdb_ref[...]
        carry = carry_ref[...]
        s_parts = []
        for i in range(pps):
            wc = _lane_cumsum(lf_refs[i][...])
            s_parts.append(_dot(qbdb, kt_refs[i][...].astype(BF16)) - tile_tokens(carry + wc))
            carry = carry + jnp.broadcast_to(wc[:, page - 1:page], wc.shape)
        carry_ref[...] = carry
        v_parts = [vt_refs[i][...].astype(BF16) for i in range(pps)]
        if with_new_keys:
            off = lax.rem(b * t, LANES)
            pos = lax.broadcasted_iota(jnp.int32, (heads, page), 1)
            lf_new = jnp.where((pos >= off) & (pos < off + t), lfnt_ref[...], 0.0)
            s_new = _dot(qbdb, knt_ref[...].astype(BF16)) - tile_tokens(carry + _lane_cumsum(lf_new))
            row = lax.broadcasted_iota(jnp.int32, s_new.shape, 0)
            col = lax.broadcasted_iota(jnp.int32, s_new.shape, 1)
            visible = (col >= off) & ((col - off) * heads <= row)
            s_parts.append(jnp.where(visible, s_new, NEG_INF))
            v_parts.append(vnt_ref[...].astype(BF16))
        s = jnp.concatenate(s_parts, axis=1)
        m_old = m_ref[...]
        m_new = jnp.maximum(m_old, jnp.max(s, axis=1, keepdims=True))
        alpha = jnp.exp(m_old - m_new)
        p = jnp.exp(s - m_new)
        l_new = alpha * l_ref[...] + jnp.sum(p, axis=1, keepdims=True)
        acc = alpha * acc_ref[...] + _dot_nt(p.astype(BF16), jnp.concatenate(v_parts, axis=1))
        m_ref[...] = m_new
        l_ref[...] = l_new
        acc_ref[...] = acc
        return l_new, acc

    last_group = j == n_steps - 1

    @pl.when(jnp.logical_not(last_group))
    def _():
        attend(False)

    @pl.when(last_group)
    def _():
        l_new, acc = attend(True)
        o = acc / l_new
        o = jnp.where(own[None], o.reshape(t, heads, w), 0.0)
        o_ref[...] = jnp.sum(o, axis=1).astype(o_ref.dtype)


def _ff_chunks(d_ff, n):
    units = d_ff // LANES
    per = -(-units // n)
    chunks = []
    lo = 0
    while lo < units:
        width = min(per, units - lo)
        chunks.append((lo * LANES, width * LANES))
        lo += width
    return tuple(chunks)


def _fox_sample_with_finish(page_table, q, knt, vnt, lfnt, cache_kt, cache_vt, cache_lft,
                            x, o_fox, yr, wof, wor, ln1_g, ln1_b, w_gate, w_up, w_down, ln2_g, ln2_b,
                            *, pps, alpha, tm):
    n_seq, t, w = q.shape
    n_pages = page_table.shape[1]
    _, heads, page = cache_lft.shape
    dh = w // heads
    n_rows, d = x.shape
    d_ff = w_gate.shape[1]
    assert page == LANES and n_pages % pps == 0 and d_ff % LANES == 0 and (n_seq * t) % LANES == 0 and LANES % t == 0
    grid = (n_seq, n_pages // pps)
    n_tiles = n_rows // tm
    total_steps = grid[0] * grid[1]
    assert n_rows % tm == 0 and total_steps % n_tiles == 0
    steps_per_tile = total_steps // n_tiles
    assert steps_per_tile >= 3
    ff_chunks = _ff_chunks(d_ff, steps_per_tile - 2)

    def seq_spec(shape):
        return pl.BlockSpec((None,) + shape, lambda b, j, pt: (b, 0, 0))

    def tile_spec(width):
        return pl.BlockSpec((tm, width), lambda b, j, pt: ((b * grid[1] + j) // steps_per_tile, 0))

    def const_spec(a):
        return pl.BlockSpec(a.shape, lambda b, j, pt: (0,) * a.ndim, pipeline_mode=pl.Buffered(1))

    hbm = pl.BlockSpec(memory_space=pl.ANY)
    consts = (wof, wor, ln1_g, ln1_b, w_gate, w_up, w_down, ln2_g, ln2_b)
    def new_keys_spec(rows):
        return pl.BlockSpec((rows, LANES), lambda b, j, pt: (0, (b * t) // LANES))

    in_specs = ([seq_spec((t, w)), new_keys_spec(w), new_keys_spec(w), new_keys_spec(heads), hbm, hbm, hbm,
                 tile_spec(d), tile_spec(o_fox.shape[1]), tile_spec(yr.shape[1])] + [const_spec(a) for a in consts])
    grid_spec = pltpu.PrefetchScalarGridSpec(
        num_scalar_prefetch=1, grid=grid, in_specs=in_specs,
        out_specs=[pl.BlockSpec((None, t, w), lambda b, j, pt: (b, 0, 0)), tile_spec(d)],
        scratch_shapes=[pltpu.VMEM((2, pps, w, page), F32), pltpu.VMEM((2, pps, w, page), F32),
                        pltpu.VMEM((2, pps, heads, page), F32), pltpu.SemaphoreType.DMA((2, 3)),
                        pltpu.VMEM((t * heads, w), BF16),
                        pltpu.VMEM((t * heads, 1), F32), pltpu.VMEM((t * heads, 1), F32),
                        pltpu.VMEM((t * heads, w), F32), pltpu.VMEM((heads, page), F32),
                        pltpu.VMEM((tm, d), F32), pltpu.VMEM((tm, d), BF16), pltpu.VMEM((tm, d), F32)])
    return pl.pallas_call(
        functools.partial(_fox_sample_kernel, pps=pps, heads=heads, dh=dh, t=t, page=page, alpha=alpha,
                          ff_chunks=ff_chunks, steps_per_tile=steps_per_tile),
        grid_spec=grid_spec,
        out_shape=[jax.ShapeDtypeStruct((n_seq, t, w), BF16), jax.ShapeDtypeStruct((n_rows, d), F32)],
        compiler_params=_params(("arbitrary", "arbitrary")),
        name="fox_sample_attention_and_prompt_finish",
    )(page_table, q, knt, vnt, lfnt, cache_kt, cache_vt, cache_lft, x, o_fox, yr, *consts)


def _finish_kernel(x_ref, of_ref, yr_ref, wof_ref, wor_ref, g1_ref, b1_ref, wg_ref, wu_ref, wd_ref,
                   g2_ref, b2_ref, y_ref, *, alpha):
    mix = _dot(of_ref[...], wof_ref[...]) + _dot(yr_ref[...], wor_ref[...])
    h = _layer_norm(alpha * x_ref[...] + mix, g1_ref[...], b1_ref[...])
    hb = h.astype(BF16)
    act = (_silu(_dot(hb, wg_ref[...])) * _dot(hb, wu_ref[...])).astype(BF16)
    f = _dot(act, wd_ref[...])
    y_ref[...] = _layer_norm(alpha * h + f, g2_ref[...], b2_ref[...])


def _finish(x, o_fox, yr, wof, wor, ln1_g, ln1_b, w_gate, w_up, w_down, ln2_g, ln2_b, *, alpha, tm):
    n, d = x.shape
    tm = min(tm, n)

    def row_spec(w):
        return pl.BlockSpec((tm, w), lambda i: (i, 0))

    def const_spec(a):
        return pl.BlockSpec(a.shape, lambda i: (0,) * a.ndim, pipeline_mode=pl.Buffered(1))

    consts = (wof, wor, ln1_g, ln1_b, w_gate, w_up, w_down, ln2_g, ln2_b)
    return pl.pallas_call(
        functools.partial(_finish_kernel, alpha=alpha),
        grid=(n // tm,),
        in_specs=[row_spec(d), row_spec(o_fox.shape[1]), row_spec(yr.shape[1])] + [const_spec(a) for a in consts],
        out_specs=row_spec(d),
        out_shape=jax.ShapeDtypeStruct((n, d), F32),
        compiler_params=_params(("parallel",)),
        name="finish",
    )(x, o_fox, yr, *consts)


def _rope_tables(pos, dk):
    half = dk // 2
    inv = ROPE_BASE ** (-jnp.arange(half, dtype=F32) / half)
    ang = pos.astype(F32)[:, None] * inv[None, :]
    cos = jnp.cos(ang)
    sin = jnp.sin(ang)
    reps = LANES // dk
    cos_row = jnp.tile(jnp.concatenate([cos, cos], axis=1), (1, reps))
    sin_row = jnp.tile(jnp.concatenate([-sin, sin], axis=1), (1, reps))
    return cos_row, sin_row


def _decay_tables(heads, chunk):
    f = np.float32
    log_g = np.log1p(-(f(2.0) ** (f(-5.0) - np.arange(heads, dtype=f)))).astype(f)
    idx = np.arange(chunk, dtype=f)
    diff = idx[:, None] - idx[None, :]
    dmat = np.where(diff[None] >= 0, np.exp(np.maximum(diff, f(0.0))[None] * log_g[:, None, None]), f(0.0)).astype(f)
    xi = np.exp((idx + f(1.0))[:, None] * log_g[None, :]).astype(f)
    zeta = np.exp((f(chunk) - f(1.0) - idx)[:, None] * log_g[None, :]).astype(f)
    gc = np.exp(f(chunk) * log_g).astype(f)
    return dmat, xi, zeta, gc


def kernel(x_prompt, x_sample, cache_k, cache_v, cache_logf, state_ret, page_table, w_in, b_fgate, ret_gn_g,
           ret_gn_b, w_o, ln1_g, ln1_b, w_gate, w_up, w_down, ln2_g, ln2_b):
    depth, d_model, _ = w_in.shape
    batch, seq, _ = x_prompt.shape
    n_seq, t_dec, _ = x_sample.shape
    _, n_pool, page, fox_heads, fox_dh = cache_k.shape
    _, _, ret_heads, ret_dk, ret_dv = state_ret.shape
    fox_w = fox_heads * fox_dh
    ret_qk_w = ret_heads * ret_dk
    ret_v_w = ret_heads * ret_dv
    past_len = page_table.shape[1] * page
    alpha = (2 * depth) ** 0.25
    o_ff = 3 * fox_w
    o_rq = o_ff + fox_heads
    assert LANES % ret_dk == 0 and ret_qk_w % LANES == 0 and seq % RET_CHUNK == 0 and t_dec < RET_CHUNK

    cos_p, sin_p = _rope_tables(jnp.arange(seq, dtype=jnp.int32), ret_dk)
    pos_s = past_len + jnp.arange(t_dec, dtype=jnp.int32)
    cos_s, sin_s = _rope_tables(jnp.tile(pos_s, n_seq), ret_dk)
    dmat_p, xi_p, zeta_p, gc_p = _decay_tables(ret_heads, RET_CHUNK)
    xi_p_tab = np.ascontiguousarray(np.broadcast_to(xi_p.T[:, :, None], (ret_heads, RET_CHUNK, ret_dv)))
    zeta_p_tab = np.repeat(zeta_p, ret_dk, axis=1)
    gc_p_static = tuple(float(v) for v in gc_p)

    seqs = 8
    while n_seq % seqs:
        seqs //= 2
    rows = seqs * t_dec
    dmat_s, xi_s, zeta_s, gc_s = _decay_tables(ret_heads, t_dec)
    r_h = np.repeat(np.arange(ret_heads), rows)
    r_s = np.tile(np.repeat(np.arange(seqs), t_dec), ret_heads)
    r_t = np.tile(np.arange(t_dec), ret_heads * seqs)
    dm_s_tab = np.where((r_h[:, None] == r_h[None, :]) & (r_s[:, None] == r_s[None, :]),
                        dmat_s[r_h[:, None], r_t[:, None], r_t[None, :]], np.float32(0.0)).astype(np.float32)
    xi_s_tab = np.ascontiguousarray(np.broadcast_to(xi_s[r_t, r_h][:, None], (ret_heads * rows, ret_dv)))
    zeta_s_tab = np.tile(np.repeat(zeta_s, ret_dk, axis=1), (seqs, 1))
    gc_s_tab = np.ascontiguousarray(np.broadcast_to(np.repeat(gc_s, ret_dk)[:, None], (ret_qk_w, ret_dv)))

    pps = 16
    while page_table.shape[1] % pps:
        pps //= 2

    hp = x_prompt
    hs = x_sample.reshape(1, n_seq * t_dec, d_model)
    kp_l, vp_l, lfp_l, sp_l, ks_l, vs_l, lfs_l, ss_l = [], [], [], [], [], [], [], []
    for i in range(depth):
        w = w_in[i]
        wfox = w[:, :o_ff].astype(BF16)
        wfg = jnp.pad(w[:, o_ff:o_rq], ((0, 0), (0, LANES - fox_heads))).astype(BF16)
        wret = w[:, o_rq:].astype(BF16)
        bfg = jnp.pad(b_fgate[i], (0, LANES - fox_heads))[None, :]
        wof = w_o[i][:fox_w].astype(BF16)
        wor = w_o[i][fox_w:].astype(BF16)
        gn_g = ret_gn_g[i][None, :]
        gn_b = ret_gn_b[i][None, :]
        tail = (wof, wor, ln1_g[i][None, :], ln1_b[i][None, :], w_gate[i].astype(BF16), w_up[i].astype(BF16),
                w_down[i].astype(BF16), ln2_g[i][None, :], ln2_b[i][None, :])
        proj = functools.partial(_projection, fox_heads=fox_heads, fox_dh=fox_dh, ret_heads=ret_heads,
                                 ret_dk=ret_dk, tm=512)

        qtb, kt, vt, kx, vtb, lft, qr, kr, vr, g = proj(hp, wfox, wfg, wret, bfg, cos_p, sin_p, prompt=True)
        o_f = _fox_prompt(qtb, kx, vtb, heads=fox_heads, dh=fox_dh, tq=1024)
        yr, s_p = _ret_prompt(qr, kr, vr, g, dmat_p, xi_p_tab, zeta_p_tab, gn_g, gn_b, heads=ret_heads,
                              dk=ret_dk, dv=ret_dv, gc=gc_p_static, tr=512)
        kp_l.append(jnp.transpose(kt.reshape(batch, fox_heads, fox_dh, seq), (0, 3, 1, 2)))
        vp_l.append(jnp.transpose(vt.reshape(batch, fox_heads, fox_dh, seq), (0, 3, 1, 2)))
        lfp_l.append(jnp.transpose(lft, (0, 2, 1)))
        sp_l.append(s_p.reshape(batch, ret_heads, ret_dk, ret_dv).astype(state_ret.dtype))
        o_f_p = o_f.reshape(batch * seq, fox_w)
        yr_p = yr.reshape(batch * seq, ret_v_w)

        qf, kf, vf, lf, knt, vnt, lfnt, qr, kr, vr, g = proj(hs, wfox, wfg, wret, bfg, cos_s, sin_s, prompt=False)
        cache_kt = jnp.transpose(cache_k[i], (0, 2, 3, 1)).reshape(n_pool, fox_w, page)
        cache_vt = jnp.transpose(cache_v[i], (0, 2, 3, 1)).reshape(n_pool, fox_w, page)
        cache_lft = jnp.transpose(cache_logf[i], (0, 2, 1))
        o_f, new_hp = _fox_sample_with_finish(
            page_table, qf.reshape(n_seq, t_dec, fox_w), knt[0], vnt[0], lfnt[0], cache_kt, cache_vt, cache_lft,
            hp.reshape(batch * seq, d_model), o_f_p, yr_p, *tail, pps=pps, alpha=alpha, tm=256)
        yr, s_s = _ret_sample(qr[0], kr[0], vr[0], g[0], state_ret[i].reshape(n_seq, ret_qk_w, ret_dv),
                              dm_s_tab, xi_s_tab, zeta_s_tab, gc_s_tab, gn_g, gn_b, heads=ret_heads, dk=ret_dk,
                              dv=ret_dv, t=t_dec, seqs=seqs)
        ks_l.append(kf.reshape(n_seq, t_dec, fox_heads, fox_dh))
        vs_l.append(vf.reshape(n_seq, t_dec, fox_heads, fox_dh))
        lfs_l.append(lf.reshape(n_seq, t_dec, fox_heads))
        ss_l.append(s_s.reshape(n_seq, ret_heads, ret_dk, ret_dv).astype(state_ret.dtype))
        hs = _finish(hs[0], o_f.reshape(n_seq * t_dec, fox_w), yr, *tail, alpha=alpha,
                     tm=512).reshape(1, n_seq * t_dec, d_model)
        hp = new_hp.reshape(batch, seq, d_model)
    return (hp, hs.reshape(n_seq, t_dec, d_model), jnp.stack(kp_l), jnp.stack(vp_l), jnp.stack(lfp_l),
            jnp.stack(sp_l), jnp.stack(ks_l), jnp.stack(vs_l), jnp.stack(lfs_l), jnp.stack(ss_l))
```
